```python
import math
import jax, jax.numpy as jnp
from jax import lax
import numpy as np

D_MODEL = 2048
BATCH = 4
SEQ = 2048
DEPTH = 2
DEC_BATCH = 128
DEC_SEQ = 8
PAST_LEN = 16384
PAGE_SIZE = 128

D_DELTA = D_MODEL // 2
D_CONF = D_MODEL - D_DELTA
HEAD_DIM = 128
N_HEADS = D_DELTA // HEAD_DIM
QKV_CONV = 4
CONF_KERNEL = 31
CHUNK = 64
EPS = 1e-6
SPLIT_SIZES = (D_DELTA, D_DELTA, D_DELTA, D_DELTA, N_HEADS, N_HEADS, D_CONF, D_CONF, D_CONF)
D_IN_PROJ = 4 * D_DELTA + 2 * N_HEADS + 3 * D_CONF

kernel_name = "hybrid_gdn_conformer_adaln_step"


def rms_norm(x, g):
    xf = x.astype(jnp.float32)
    return xf * lax.rsqrt(jnp.mean(xf * xf, axis=-1, keepdims=True) + EPS) * g.astype(jnp.float32)


def layer_norm(x, g, b):
    xf = x.astype(jnp.float32)
    mu = jnp.mean(xf, axis=-1, keepdims=True)
    xc = xf - mu
    var = jnp.mean(xc * xc, axis=-1, keepdims=True)
    return xc * lax.rsqrt(var + EPS) * g.astype(jnp.float32) + b.astype(jnp.float32)


def l2_normalize(x):
    xf = x.astype(jnp.float32)
    return xf * lax.rsqrt(jnp.sum(xf * xf, axis=-1, keepdims=True) + EPS)


def split_columns(p):
    outs, start = [], 0
    for size in SPLIT_SIZES:
        outs.append(p[..., start:start + size])
        start += size
    return outs


def causal_dwconv(x, buf, w):
    K, C = w.shape
    xp = jnp.concatenate([buf.astype(x.dtype), x], axis=1)
    out = lax.conv_general_dilated(
        xp, w.astype(x.dtype)[:, None, :], window_strides=(1,), padding='VALID',
        dimension_numbers=('NWC', 'WIO', 'NWC'), feature_group_count=C)
    new_buf = xp[:, xp.shape[1] - (K - 1):]
    return out, new_buf


def gated_delta_rule(q, k, v, g, beta, S0):
    B, T, H, dk = q.shape
    dv = v.shape[-1]
    C = math.gcd(T, CHUNK)
    N = T // C

    def to_chunks(t):
        return t.reshape(B, N, C, H, t.shape[-1]).transpose(1, 0, 3, 2, 4)

    q = to_chunks(q) * (dk ** -0.5)
    k = to_chunks(k)
    v = to_chunks(v.astype(jnp.float32))
    g = g.reshape(B, N, C, H).transpose(1, 0, 3, 2)
    beta = beta.reshape(B, N, C, H).transpose(1, 0, 3, 2)
    gc = jnp.cumsum(g, axis=-1)
    causal = jnp.tril(jnp.ones((C, C), dtype=bool))
    strict = jnp.tril(jnp.ones((C, C), dtype=bool), -1)
    diff = gc[..., :, None] - gc[..., None, :]
    decay = jnp.exp(jnp.where(causal, diff, -jnp.inf))
    kk = jnp.einsum('nbhtd,nbhsd->nbhts', k, k)
    L = jnp.where(strict, beta[..., :, None] * kk * decay, 0.0)
    a_mat = jnp.eye(C, dtype=jnp.float32) + L
    rhs = jnp.concatenate([beta[..., None] * v, (beta * jnp.exp(gc))[..., None] * k], axis=-1)
    sol = lax.linalg.triangular_solve(a_mat, rhs, left_side=True, lower=True)
    Uv, W = sol[..., :dv], sol[..., dv:]
    attn = jnp.einsum('nbhtd,nbhsd->nbhts', q, k) * decay
    qG = q * jnp.exp(gc)[..., None]
    kdec = k * jnp.exp(gc[..., -1:] - gc)[..., None]
    gtot = jnp.exp(gc[..., -1])

    def step(S, xs):
        Uv_n, W_n, attn_n, qG_n, kdec_n, gt_n = xs
        U = Uv_n - jnp.einsum('bhck,bhkv->bhcv', W_n, S)
        o = jnp.einsum('bhck,bhkv->bhcv', qG_n, S) + jnp.einsum('bhcs,bhsv->bhcv', attn_n, U)
        S = gt_n[..., None, None] * S + jnp.einsum('bhck,bhcv->bhkv', kdec_n, U)
        return S, o

    S, o = lax.scan(step, S0.astype(jnp.float32), (Uv, W, attn, qG, kdec, gtot))
    o = o.transpose(1, 0, 3, 2, 4).reshape(B, T, H, dv)
    return o, S


def mixer_layer(x, c, S0, qkv_buf, glu_buf, norm_g, w_ada, b_ada, w_in, w_qkv_conv,
                a_log, dt_bias, head_norm_g, w_dw, b_dw, ln_g, ln_b, w_out):
    B, T, D = x.shape
    mod = jnp.dot(jax.nn.silu(c), w_ada) + b_ada
    shift, scale, gate = mod[:, :D], mod[:, D:2 * D], mod[:, 2 * D:]
    h = (rms_norm(x, norm_g) * (1.0 + scale[:, None, :]) + shift[:, None, :]).astype(x.dtype)
    proj = jnp.einsum('btd,de->bte', h, w_in)
    q, k, v, z_d, b_raw, a_raw, glu_a, glu_b, z_c = split_columns(proj)

    qkv = jnp.concatenate([q, k, v], axis=-1)
    qkv, new_qkv_buf = causal_dwconv(qkv, qkv_buf, w_qkv_conv)
    qkv = jax.nn.silu(qkv)
    q = l2_normalize(qkv[..., :D_DELTA].reshape(B, T, N_HEADS, HEAD_DIM))
    k = l2_normalize(qkv[..., D_DELTA:2 * D_DELTA].reshape(B, T, N_HEADS, HEAD_DIM))
    v = qkv[..., 2 * D_DELTA:].reshape(B, T, N_HEADS, HEAD_DIM)
    beta = jax.nn.sigmoid(b_raw.astype(jnp.float32))
    g = -jnp.exp(a_log.astype(jnp.float32)) * jax.nn.softplus(a_raw.astype(jnp.float32) + dt_bias.astype(jnp.float32))
    o, S_new = gated_delta_rule(q, k, v, g, beta, S0)
    o = rms_norm(o, head_norm_g).reshape(B, T, D_DELTA)
    o = (o * jax.nn.silu(z_d.astype(jnp.float32))).astype(x.dtype)

    u = glu_a * jax.nn.sigmoid(glu_b)
    u, new_glu_buf = causal_dwconv(u, glu_buf, w_dw)
    u = u + b_dw
    u = jax.nn.silu(layer_norm(u, ln_g, ln_b))
    u = (u * jax.nn.silu(z_c.astype(jnp.float32))).astype(x.dtype)

    mix = jnp.einsum('bte,ed->btd', jnp.concatenate([o, u], axis=-1), w_out)
    x = x + (gate[:, None, :] * mix).astype(x.dtype)
    return x, S_new, new_qkv_buf, new_glu_buf


def setup_inputs(seed: int = 0) -> dict:
    key = jax.random.key(seed)
    ks = jax.random.split(key, 24)
    f32 = jnp.float32
    D = D_MODEL
    x_prompt = jax.random.normal(ks[0], (BATCH, SEQ, D), f32)
    x_sample = jax.random.normal(ks[1], (DEC_BATCH, DEC_SEQ, D), f32)
    c_prompt = jax.random.normal(ks[2], (BATCH, D), f32)
    c_sample = jax.random.normal(ks[3], (DEC_BATCH, D), f32)
    state_delta = 0.1 * jax.random.normal(ks[4], (DEPTH, DEC_BATCH, N_HEADS, HEAD_DIM, HEAD_DIM), f32)
    state_qkv_conv = jax.random.normal(ks[5], (DEPTH, DEC_BATCH, QKV_CONV - 1, 3 * D_DELTA), f32)
    state_glu_conv = 0.5 * jax.random.normal(ks[6], (DEPTH, DEC_BATCH, CONF_KERNEL - 1, D_CONF), f32)
    norm_g = 1.0 + 0.02 * jax.random.normal(ks[7], (DEPTH, D), f32)
    w_ada = 0.5 * (D ** -0.5) * jax.random.normal(ks[8], (DEPTH, D, 3 * D), f32)
    ada_base = jnp.concatenate([jnp.zeros((2 * D,), f32), jnp.ones((D,), f32)])
    b_ada = ada_base + 0.02 * jax.random.normal(ks[9], (DEPTH, 3 * D), f32)
    w_in = (D ** -0.5) * jax.random.normal(ks[10], (DEPTH, D, D_IN_PROJ), f32)
    w_qkv_conv = (QKV_CONV ** -0.5) * jax.random.normal(ks[11], (DEPTH, QKV_CONV, 3 * D_DELTA), f32)
    a_log = jnp.log(jax.random.uniform(ks[12], (DEPTH, N_HEADS), f32, 1.0, 16.0))
    dt = jnp.exp(jax.random.uniform(ks[13], (DEPTH, N_HEADS), f32, math.log(1e-3), math.log(1e-1)))
    dt_bias = dt + jnp.log(-jnp.expm1(-dt))
    head_norm_g = 1.0 + 0.02 * jax.random.normal(ks[14], (DEPTH, HEAD_DIM), f32)
    w_dw = (CONF_KERNEL ** -0.5) * jax.random.normal(ks[15], (DEPTH, CONF_KERNEL, D_CONF), f32)
    b_dw = 0.02 * jax.random.normal(ks[16], (DEPTH, D_CONF), f32)
    ln_g = 1.0 + 0.02 * jax.random.normal(ks[17], (DEPTH, D_CONF), f32)
    ln_b = 0.02 * jax.random.normal(ks[18], (DEPTH, D_CONF), f32)
    w_out = (D ** -0.5) * jax.random.normal(ks[19], (DEPTH, D_DELTA + D_CONF, D), f32)
    final_g = 1.0 + 0.02 * jax.random.normal(ks[20], (D,), f32)
    return {"x_prompt": x_prompt, "x_sample": x_sample, "c_prompt": c_prompt, "c_sample": c_sample,
            "state_delta": state_delta, "state_qkv_conv": state_qkv_conv, "state_glu_conv": state_glu_conv,
            "norm_g": norm_g, "w_ada": w_ada, "b_ada": b_ada, "w_in": w_in, "w_qkv_conv": w_qkv_conv,
            "a_log": a_log, "dt_bias": dt_bias, "head_norm_g": head_norm_g, "w_dw": w_dw, "b_dw": b_dw,
            "ln_g": ln_g, "ln_b": ln_b, "w_out": w_out, "final_g": final_g}


def reference(x_prompt, x_sample, c_prompt, c_sample, state_delta, state_qkv_conv, state_glu_conv,
              norm_g, w_ada, b_ada, w_in, w_qkv_conv, a_log, dt_bias, head_norm_g, w_dw, b_dw,
              ln_g, ln_b, w_out, final_g):
    Bp = x_prompt.shape[0]
    hp, hs = x_prompt, x_sample
    Sp_l, qp_l, gp_l, Ss_l, qs_l, gs_l = [], [], [], [], [], []
    for l in range(DEPTH):
        params = (norm_g[l], w_ada[l], b_ada[l], w_in[l], w_qkv_conv[l], a_log[l], dt_bias[l],
                  head_norm_g[l], w_dw[l], b_dw[l], ln_g[l], ln_b[l], w_out[l])
        S0p = jnp.zeros((Bp, N_HEADS, HEAD_DIM, HEAD_DIM), jnp.float32)
        qb0 = jnp.zeros((Bp, QKV_CONV - 1, 3 * D_DELTA), x_prompt.dtype)
        gb0 = jnp.zeros((Bp, CONF_KERNEL - 1, D_CONF), x_prompt.dtype)
        hp, Sp, qp, gp = mixer_layer(hp, c_prompt, S0p, qb0, gb0, *params)
        hs, Ss, qs, gs = mixer_layer(hs, c_sample, state_delta[l], state_qkv_conv[l], state_glu_conv[l], *params)
        Sp_l.append(Sp); qp_l.append(qp); gp_l.append(gp)
        Ss_l.append(Ss); qs_l.append(qs); gs_l.append(gs)
    y_prompt = rms_norm(hp, final_g).astype(x_prompt.dtype)
    y_sample = rms_norm(hs, final_g).astype(x_sample.dtype)
    return (y_prompt, y_sample,
            jnp.stack(Sp_l), jnp.stack(qp_l), jnp.stack(gp_l),
            jnp.stack(Ss_l), jnp.stack(qs_l), jnp.stack(gs_l))
```

```python
import functools

import jax
import jax.numpy as jnp
from jax import lax
from jax.experimental import pallas as pl
from jax.experimental.pallas import tpu as pltpu

F32 = jnp.float32
BF16 = jnp.bfloat16

EPS = 1e-6
QKV_CONV = 4
CONF_KERNEL = 31
CHUNK = 64
LANE = 128
SUBLANE = 8
QKV_HIST = SUBLANE
GLU_HIST = 32
VMEM_LIMIT_BYTES = 56 * 1024 * 1024


def _params(n_axes):
    return pltpu.CompilerParams(dimension_semantics=("arbitrary",) * n_axes,
                                vmem_limit_bytes=VMEM_LIMIT_BYTES)


def _sigmoid(x):
    return 1.0 / (1.0 + jnp.exp(-x))


def _silu(x):
    return x * _sigmoid(x)


def _softplus(x):
    return jnp.maximum(x, 0.0) + jnp.log1p(jnp.exp(-jnp.abs(x)))


def _dot(a, b):
    return jnp.dot(a, b, preferred_element_type=F32)


def _dot_nt(a, b):
    return lax.dot_general(a, b, (((1,), (1,)), ((), ())), preferred_element_type=F32)


def _dot_tn(a, b):
    return lax.dot_general(a, b, (((0,), (0,)), ((), ())), preferred_element_type=F32)


def _split(a):
    hi = a.astype(BF16)
    lo = (a - hi.astype(F32)).astype(BF16)
    return hi, lo


def _dot_f32(a, b, dot=_dot):
    ah, al = _split(a)
    bh, bl = _split(b)
    return dot(ah, bh) + (dot(ah, bl) + dot(al, bh))


def _inv_unit_lower(low, n):
    rows = lax.broadcasted_iota(jnp.int32, (n, n), 0)
    cols = lax.broadcasted_iota(jnp.int32, (n, n), 1)
    inv = jnp.where(rows == cols, 1.0, 0.0) - low
    power = low
    m = 2
    while m < n:
        power = _dot_f32(power, power)
        inv = inv + _dot_f32(inv, power)
        m *= 2
    return inv


def _ada_kernel(c_ref, w_ref, b_ref, o_ref):
    s = _silu(c_ref[...]).astype(BF16)
    o_ref[0] = _dot(s, w_ref[0].astype(BF16)) + b_ref[0]


def _ada(c_all, w_ada, b_ada):
    depth, d, n = w_ada.shape
    rows = c_all.shape[0]
    tn = min(n, 512)
    return pl.pallas_call(
        _ada_kernel,
        out_shape=jax.ShapeDtypeStruct((depth, rows, n), F32),
        grid=(depth, n // tn),
        in_specs=[pl.BlockSpec((rows, d), lambda l, j: (0, 0)),
                  pl.BlockSpec((1, d, tn), lambda l, j: (l, 0, j)),
                  pl.BlockSpec((1, 1, tn), lambda l, j: (l, 0, j))],
        out_specs=pl.BlockSpec((1, rows, tn), lambda l, j: (l, 0, j)),
        compiler_params=_params(2),
        name="ada_mod",
    )(c_all, w_ada, b_ada.reshape(depth, 1, n))


def _inproj_kernel(x_ref, shift_ref, scale_ref, g_ref, w_ref, wba_ref, proj_ref, ba_ref, h_ref):
    @pl.when(pl.program_id(2) == 0)
    def _():
        x = x_ref[...]
        ms = jnp.mean(x * x, axis=-1, keepdims=True)
        h = x * lax.rsqrt(ms + EPS) * g_ref[...] * (1.0 + scale_ref[...]) + shift_ref[...]
        h = h.reshape(h_ref.shape).astype(BF16)
        h_ref[...] = h
        ba_ref[...] = _dot(h, wba_ref[...])

    proj_ref[...] = _dot(h_ref[...], w_ref[...])


def _inproj(x, mod, norm_g, w_main, w_ba, *, bb, tt, tn):
    b, t, d = x.shape
    n = w_main.shape[1]
    nb, nt, nn = b // bb, t // tt, n // tn
    tm = bb * tt
    return pl.pallas_call(
        _inproj_kernel,
        out_shape=(jax.ShapeDtypeStruct((b * t, n), F32),
                   jax.ShapeDtypeStruct((b * t, LANE), F32)),
        grid=(nb, nt, nn),
        in_specs=[pl.BlockSpec((bb, tt, d), lambda i, j, k: (i, j, 0)),
                  pl.BlockSpec((bb, 1, d), lambda i, j, k: (i, 0, 0)),
                  pl.BlockSpec((bb, 1, d), lambda i, j, k: (i, 0, 1)),
                  pl.BlockSpec((1, d), lambda i, j, k: (0, 0)),
                  pl.BlockSpec((d, tn), lambda i, j, k: (0, k)),
                  pl.BlockSpec((d, LANE), lambda i, j, k: (0, 0))],
        out_specs=(pl.BlockSpec((tm, tn), lambda i, j, k: (i * nt + j, k)),
                   pl.BlockSpec((tm, LANE), lambda i, j, k: (i * nt + j, 0))),
        scratch_shapes=[pltpu.VMEM((tm, d), BF16)],
        compiler_params=_params(3),
        name="in_proj",
    )(x, mod, mod, norm_g.reshape(1, d), w_main, w_ba)


def _short_conv(ext_ref, wconv_ref, rows):
    first = QKV_HIST - (QKV_CONV - 1)
    acc = wconv_ref[0:1, :] * ext_ref[first:first + rows, :]
    for j in range(1, QKV_CONV):
        acc = acc + wconv_ref[j:j + 1, :] * ext_ref[first + j:first + j + rows, :]
    return _silu(acc)


def _gates(ba, alog, dtb):
    beta = _sigmoid(ba)
    g = -jnp.exp(alog) * _softplus(ba + dtb)
    return beta, g


def _delta_chunk(q, k, v, z, beta_col, gcol, grow, s_prev, hng, c):
    dk = q.shape[-1]
    rows = lax.broadcasted_iota(jnp.int32, (c, c), 0)
    cols = lax.broadcasted_iota(jnp.int32, (c, c), 1)
    causal = rows >= cols
    strict = rows > cols
    q = q * lax.rsqrt(jnp.sum(q * q, axis=-1, keepdims=True) + EPS) * (dk ** -0.5)
    k = k * lax.rsqrt(jnp.sum(k * k, axis=-1, keepdims=True) + EPS)
    decay = jnp.where(causal, jnp.exp(jnp.where(causal, gcol - grow, 0.0)), 0.0)
    kb = k.astype(BF16)
    kk = _dot_nt(kb, kb)
    low = jnp.where(strict, beta_col * kk * decay, 0.0)
    inv = _inv_unit_lower(low, c)
    egc = jnp.exp(gcol)
    rhs = jnp.concatenate([beta_col * v, (beta_col * egc) * k], axis=-1)
    sol = _dot_f32(inv, rhs)
    uv, w = sol[:, :dk], sol[:, dk:]
    attn = _dot_nt(q.astype(BF16), kb) * decay
    glast = gcol[c - 1:c, :]
    qg = q * egc
    kdec = k * jnp.exp(glast - gcol)
    sb = s_prev.astype(BF16)
    u = uv - _dot(w.astype(BF16), sb)
    ub = u.astype(BF16)
    o = _dot(qg.astype(BF16), sb) + _dot(attn.astype(BF16), ub)
    s_new = jnp.exp(glast) * s_prev + _dot_tn(kdec.astype(BF16), ub)
    o = o * lax.rsqrt(jnp.mean(o * o, axis=-1, keepdims=True) + EPS) * hng
    return o * _silu(z), s_new


def _cumsum_rows(g, c):
    rows = lax.broadcasted_iota(jnp.int32, (c, c), 0)
    cols = lax.broadcasted_iota(jnp.int32, (c, c), 1)
    lower = jnp.where(rows >= cols, 1.0, 0.0).astype(BF16)
    upper = jnp.where(rows <= cols, 1.0, 0.0).astype(BF16)
    g0 = g.astype(BF16)
    r1 = g - g0.astype(F32)
    g1 = r1.astype(BF16)
    g2 = (r1 - g1.astype(F32)).astype(BF16)
    col = _dot(lower, g0) + (_dot(lower, g1) + _dot(lower, g2))
    row = _dot_tn(g0, upper) + (_dot_tn(g1, upper) + _dot_tn(g2, upper))
    return col, row


def _delta_prompt_kernel(qkv_ref, z_ref, ba_ref, wconv_ref, alog_ref, dtb_ref, hng_ref,
                         o_ref, s_ref, qbuf_ref, ext_ref, *, c, heads, dk):
    i = pl.program_id(1)
    dd = heads * dk

    @pl.when(i == 0)
    def _():
        ext_ref[0:QKV_HIST, :] = jnp.zeros((QKV_HIST, ext_ref.shape[1]), F32)
        s_ref[...] = jnp.zeros(s_ref.shape, F32)

    ext_ref[QKV_HIST:QKV_HIST + c, :] = qkv_ref[...]
    qkv = _short_conv(ext_ref, wconv_ref, c)
    ext_ref[0:QKV_HIST, :] = qkv_ref[c - QKV_HIST:c, :]

    beta, g = _gates(ba_ref[...], alog_ref[...], dtb_ref[...])
    gc, gct = _cumsum_rows(g, c)
    hng = hng_ref[...]
    for h in range(heads):
        o, s_new = _delta_chunk(
            qkv[:, h * dk:(h + 1) * dk], qkv[:, dd + h * dk:dd + (h + 1) * dk],
            qkv[:, 2 * dd + h * dk:2 * dd + (h + 1) * dk], z_ref[:, h * dk:(h + 1) * dk],
            beta[:, h:h + 1], gc[:, heads + h:heads + h + 1], gct[heads + h:heads + h + 1, :],
            s_ref[0, h], hng, c)
        o_ref[:, h * dk:(h + 1) * dk] = o.astype(o_ref.dtype)
        s_ref[0, h] = s_new

    @pl.when(i == pl.num_programs(1) - 1)
    def _():
        qbuf_ref[0] = qkv_ref[c - (QKV_CONV - 1):c, :]


def _delta_prompt(proj, ba, wconv, alog, dtb, hng, *, b, t, heads, dk):
    dd = heads * dk
    c = min(CHUNK, t)
    nc = t // c
    kern = functools.partial(_delta_prompt_kernel, c=c, heads=heads, dk=dk)
    return pl.pallas_call(
        kern,
        out_shape=(jax.ShapeDtypeStruct((b * t, dd), BF16),
                   jax.ShapeDtypeStruct((b, heads, dk, dk), F32),
                   jax.ShapeDtypeStruct((b, QKV_CONV - 1, 3 * dd), F32)),
        grid=(b, nc),
        in_specs=[pl.BlockSpec((c, 3 * dd), lambda i, j: (i * nc + j, 0)),
                  pl.BlockSpec((c, dd), lambda i, j: (i * nc + j, 3)),
                  pl.BlockSpec((c, LANE), lambda i, j: (i * nc + j, 0)),
                  pl.BlockSpec((QKV_CONV, 3 * dd), lambda i, j: (0, 0)),
                  pl.BlockSpec((1, LANE), lambda i, j: (0, 0)),
                  pl.BlockSpec((1, LANE), lambda i, j: (0, 0)),
                  pl.BlockSpec((1, dk), lambda i, j: (0, 0))],
        out_specs=(pl.BlockSpec((c, dd), lambda i, j: (i * nc + j, 0)),
                   pl.BlockSpec((1, heads, dk, dk), lambda i, j: (i, 0, 0, 0)),
                   pl.BlockSpec((1, QKV_CONV - 1, 3 * dd), lambda i, j: (i, 0, 0))),
        scratch_shapes=[pltpu.VMEM((QKV_HIST + c, 3 * dd), F32)],
        compiler_params=_params(2),
        name="delta_prompt",
    )(proj, proj, ba, wconv, alog, dtb, hng)


def _delta_sample_kernel(qkv_ref, z_ref, ba_ref, hist_ref, s0_ref, wconv_ref, alog_ref, dtb_ref,
                         hng_ref, o_ref, s_ref, qbuf_ref, ext_ref, *, t, heads, dk):
    dd = heads * dk
    ext_ref[0:QKV_HIST, :] = hist_ref[0]
    ext_ref[QKV_HIST:QKV_HIST + t, :] = qkv_ref[...]
    qkv = _short_conv(ext_ref, wconv_ref, t)
    qbuf_ref[0] = ext_ref[QKV_HIST + t - (QKV_CONV - 1):QKV_HIST + t, :]

    beta, g = _gates(ba_ref[...], alog_ref[...], dtb_ref[...])
    gc, gct = _cumsum_rows(g, t)
    hng = hng_ref[...]
    for h in range(heads):
        o, s_new = _delta_chunk(
            qkv[:, h * dk:(h + 1) * dk], qkv[:, dd + h * dk:dd + (h + 1) * dk],
            qkv[:, 2 * dd + h * dk:2 * dd + (h + 1) * dk], z_ref[:, h * dk:(h + 1) * dk],
            beta[:, h:h + 1], gc[:, heads + h:heads + h + 1], gct[heads + h:heads + h + 1, :],
            s0_ref[0, 0, h], hng, t)
        o_ref[:, h * dk:(h + 1) * dk] = o.astype(o_ref.dtype)
        s_ref[0, h] = s_new


def _delta_sample(proj, ba, hist, state, layer, wconv, alog, dtb, hng, *, b, t, heads, dk):
    dd = heads * dk
    kern = functools.partial(_delta_sample_kernel, t=t, heads=heads, dk=dk)
    return pl.pallas_call(
        kern,
        out_shape=(jax.ShapeDtypeStruct((b * t, dd), F32),
                   jax.ShapeDtypeStruct((b, heads, dk, dk), F32),
                   jax.ShapeDtypeStruct((b, QKV_CONV - 1, 3 * dd), F32)),
        grid=(b,),
        in_specs=[pl.BlockSpec((t, 3 * dd), lambda i: (i, 0)),
                  pl.BlockSpec((t, dd), lambda i: (i, 3)),
                  pl.BlockSpec((t, LANE), lambda i: (i, 0)),
                  pl.BlockSpec((1, QKV_HIST, 3 * dd), lambda i: (i, 0, 0)),
                  pl.BlockSpec((1, 1, heads, dk, dk), lambda i: (layer, i, 0, 0, 0)),
                  pl.BlockSpec((QKV_CONV, 3 * dd), lambda i: (0, 0)),
                  pl.BlockSpec((1, LANE), lambda i: (0, 0)),
                  pl.BlockSpec((1, LANE), lambda i: (0, 0)),
                  pl.BlockSpec((1, dk), lambda i: (0, 0))],
        out_specs=(pl.BlockSpec((t, dd), lambda i: (i, 0)),
                   pl.BlockSpec((1, heads, dk, dk), lambda i: (i, 0, 0, 0)),
                   pl.BlockSpec((1, QKV_CONV - 1, 3 * dd), lambda i: (i, 0, 0))),
        scratch_shapes=[pltpu.VMEM((QKV_HIST + t, 3 * dd), F32)],
        compiler_params=_params(1),
        name="delta_sample",
    )(proj, proj, ba, hist, state, wconv, alog, dtb, hng)


def _conf_epilogue(conv, zc, bdw, lng, lnb):
    y = conv + bdw
    mu = jnp.mean(y, axis=-1, keepdims=True)
    yc = y - mu
    var = jnp.mean(yc * yc, axis=-1, keepdims=True)
    y = yc * lax.rsqrt(var + EPS) * lng + lnb
    return _silu(y) * _silu(zc)


def _conf_prompt_kernel(ga_ref, gb_ref, zc_ref, w_ref, bdw_ref, lng_ref, lnb_ref,
                        u_ref, gbuf_ref, ext_ref, sh_ref, conv_ref, *, tt, rr):
    i = pl.program_id(1)
    dc = ext_ref.shape[1]
    first = GLU_HIST - (CONF_KERNEL - 1)

    @pl.when(i == 0)
    def _():
        ext_ref[0:GLU_HIST, :] = jnp.zeros((GLU_HIST, dc), F32)
        ext_ref[GLU_HIST + tt:GLU_HIST + tt + SUBLANE, :] = jnp.zeros((SUBLANE, dc), F32)

    ext_ref[GLU_HIST:GLU_HIST + tt, :] = ga_ref[...] * _sigmoid(gb_ref[...])
    for p in range(SUBLANE):
        sh_ref[p] = ext_ref[p:p + GLU_HIST + tt, :]

    for lt in range(dc // LANE):
        lanes = slice(lt * LANE, (lt + 1) * LANE)
        taps = [jnp.broadcast_to(w_ref[j:j + 1, lanes], (rr, LANE)) for j in range(CONF_KERNEL)]

        def body(r, carry, lanes=lanes, taps=taps):
            r0 = pl.multiple_of(r * rr, rr)
            acc = None
            for j in range(CONF_KERNEL):
                off = first + j
                win = sh_ref[off % SUBLANE, pl.ds(r0 + (off // SUBLANE) * SUBLANE, rr), lanes]
                acc = taps[j] * win if acc is None else acc + taps[j] * win
            conv_ref[pl.ds(r0, rr), lanes] = acc
            return carry

        lax.fori_loop(0, tt // rr, body, 0)

    y = _conf_epilogue(conv_ref[...], zc_ref[...], bdw_ref[...], lng_ref[...], lnb_ref[...])
    u_ref[...] = y.astype(u_ref.dtype)
    ext_ref[0:GLU_HIST, :] = ext_ref[tt:tt + GLU_HIST, :]

    @pl.when(i == pl.num_programs(1) - 1)
    def _():
        gbuf_ref[0] = ext_ref[first:GLU_HIST, :]


def _conf_prompt(proj, w_dw, b_dw, ln_g, ln_b, *, b, t, dd, dc, tt):
    nt = t // tt
    rr = min(32, tt)
    col0 = (4 * dd) // dc
    kern = functools.partial(_conf_prompt_kernel, tt=tt, rr=rr)
    row = lambda i, j: i * nt + j
    return pl.pallas_call(
        kern,
        out_shape=(jax.ShapeDtypeStruct((b * t, dc), BF16),
                   jax.ShapeDtypeStruct((b, CONF_KERNEL - 1, dc), F32)),
        grid=(b, nt),
        in_specs=[pl.BlockSpec((tt, dc), lambda i, j: (row(i, j), col0)),
                  pl.BlockSpec((tt, dc), lambda i, j: (row(i, j), col0 + 1)),
                  pl.BlockSpec((tt, dc), lambda i, j: (row(i, j), col0 + 2)),
                  pl.BlockSpec((CONF_KERNEL, dc), lambda i, j: (0, 0)),
                  pl.BlockSpec((1, dc), lambda i, j: (0, 0)),
                  pl.BlockSpec((1, dc), lambda i, j: (0, 0)),
                  pl.BlockSpec((1, dc), lambda i, j: (0, 0))],
        out_specs=(pl.BlockSpec((tt, dc), lambda i, j: (row(i, j), 0)),
                   pl.BlockSpec((1, CONF_KERNEL - 1, dc), lambda i, j: (i, 0, 0))),
        scratch_shapes=[pltpu.VMEM((GLU_HIST + tt + SUBLANE, dc), F32),
                        pltpu.VMEM((SUBLANE, GLU_HIST + tt, dc), F32),
                        pltpu.VMEM((tt, dc), F32)],
        compiler_params=_params(2),
        name="conf_prompt",
    )(proj, proj, proj, w_dw, b_dw, ln_g, ln_b)


def _conf_sample_kernel(ga_ref, gb_ref, zc_ref, hist_ref, w_ref, bdw_ref, lng_ref, lnb_ref,
                        u_ref, gbuf_ref, ext_ref, *, t, bb):
    first = GLU_HIST - (CONF_KERNEL - 1)
    u_all = ga_ref[...] * _sigmoid(gb_ref[...])
    for s in range(bb):
        ext_ref[s, 0:GLU_HIST, :] = hist_ref[s]
        ext_ref[s, GLU_HIST:GLU_HIST + t, :] = u_all[s * t:(s + 1) * t, :]
    for s in range(bb):
        acc = w_ref[0:1, :] * ext_ref[s, first:first + t, :]
        for j in range(1, CONF_KERNEL):
            acc = acc + w_ref[j:j + 1, :] * ext_ref[s, first + j:first + j + t, :]
        y = _conf_epilogue(acc, zc_ref[s * t:(s + 1) * t, :], bdw_ref[...], lng_ref[...], lnb_ref[...])
        u_ref[s * t:(s + 1) * t, :] = y.astype(u_ref.dtype)
        gbuf_ref[s] = ext_ref[s, first + t:GLU_HIST + t, :]


def _conf_sample(proj, hist, w_dw, b_dw, ln_g, ln_b, *, b, t, dd, dc, bb):
    col0 = (4 * dd) // dc
    kern = functools.partial(_conf_sample_kernel, t=t, bb=bb)
    return pl.pallas_call(
        kern,
        out_shape=(jax.ShapeDtypeStruct((b * t, dc), F32),
                   jax.ShapeDtypeStruct((b, CONF_KERNEL - 1, dc), F32)),
        grid=(b // bb,),
        in_specs=[pl.BlockSpec((bb * t, dc), lambda i: (i, col0)),
                  pl.BlockSpec((bb * t, dc), lambda i: (i, col0 + 1)),
                  pl.BlockSpec((bb * t, dc), lambda i: (i, col0 + 2)),
                  pl.BlockSpec((bb, GLU_HIST, dc), lambda i: (i, 0, 0)),
                  pl.BlockSpec((CONF_KERNEL, dc), lambda i: (0, 0)),
                  pl.BlockSpec((1, dc), lambda i: (0, 0)),
                  pl.BlockSpec((1, dc), lambda i: (0, 0)),
                  pl.BlockSpec((1, dc), lambda i: (0, 0))],
        out_specs=(pl.BlockSpec((bb * t, dc), lambda i: (i, 0)),
                   pl.BlockSpec((bb, CONF_KERNEL - 1, dc), lambda i: (i, 0, 0))),
        scratch_shapes=[pltpu.VMEM((bb, GLU_HIST + t, dc), F32)],
        compiler_params=_params(1),
        name="conf_sample",
    )(proj, proj, proj, hist, w_dw, b_dw, ln_g, ln_b)


def _outproj_kernel(x_ref, gate_ref, o_ref, u_ref, wo_ref, wu_ref, fg_ref, y_ref, *, final):
    mix = _dot(o_ref[...].astype(BF16), wo_ref[...]) + _dot(u_ref[...].astype(BF16), wu_ref[...])
    x = x_ref[...]
    y = x + gate_ref[...] * mix.reshape(x.shape)
    if final:
        y = y * lax.rsqrt(jnp.mean(y * y, axis=-1, keepdims=True) + EPS) * fg_ref[...]
    y_ref[...] = y


def _outproj(x, mod, o, u, w_o, w_u, final_g, *, bb, tt, final):
    b, t, d = x.shape
    nt = t // tt
    tm = bb * tt
    kern = functools.partial(_outproj_kernel, final=final)
    return pl.pallas_call(
        kern,
        out_shape=jax.ShapeDtypeStruct((b, t, d), F32),
        grid=(b // bb, nt),
        in_specs=[pl.BlockSpec((bb, tt, d), lambda i, j: (i, j, 0)),
                  pl.BlockSpec((bb, 1, d), lambda i, j: (i, 0, 2)),
                  pl.BlockSpec((tm, o.shape[1]), lambda i, j: (i * nt + j, 0)),
                  pl.BlockSpec((tm, u.shape[1]), lambda i, j: (i * nt + j, 0)),
                  pl.BlockSpec(w_o.shape, lambda i, j: (0, 0)),
                  pl.BlockSpec(w_u.shape, lambda i, j: (0, 0)),
                  pl.BlockSpec((1, d), lambda i, j: (0, 0))],
        out_specs=pl.BlockSpec((bb, tt, d), lambda i, j: (i, j, 0)),
        compiler_params=_params(2),
        name="out_proj",
    )(x, mod, o, u, w_o, w_u, final_g.reshape(1, d))


def _lane_row(vec, offset):
    return jnp.pad(vec.astype(F32), (offset, LANE - offset - vec.shape[0])).reshape(1, LANE)


def kernel(x_prompt, x_sample, c_prompt, c_sample, state_delta, state_qkv_conv, state_glu_conv,
           norm_g, w_ada, b_ada, w_in, w_qkv_conv, a_log, dt_bias, head_norm_g, w_dw, b_dw,
           ln_g, ln_b, w_out, final_g):
    depth = w_in.shape[0]
    bp, tp, d = x_prompt.shape
    bs, ts, _ = x_sample.shape
    heads = a_log.shape[1]
    dd = w_qkv_conv.shape[2] // 3
    dk = dd // heads
    dc = w_dw.shape[2]
    assert ts % SUBLANE == 0 and ts >= QKV_CONV - 1 and ts <= CONF_KERNEL - 1
    assert 2 * heads <= LANE and dd % dc == 0 and tp % CHUNK == 0

    mod = _ada(jnp.concatenate([c_prompt, c_sample], axis=0), w_ada, b_ada)

    tn = min(1024, dd, dc)
    tm = 1024
    tt_in = min(tm, tp)
    bb_s = min(bs, tm // ts)
    tt_out = min(512, tp)
    bb_out = min(bs, 512 // ts)
    tt_conf = min(256, tp)
    bb_conf = min(bs, 8)

    hp, hs = x_prompt, x_sample
    sp_l, qp_l, gp_l, ss_l, qs_l, gs_l = [], [], [], [], [], []
    for l in range(depth):
        mod_p = mod[l, :bp].reshape(bp, 1, 3 * d)
        mod_s = mod[l, bp:].reshape(bs, 1, 3 * d)
        w_l = w_in[l]
        w_main = jnp.concatenate([w_l[:, :4 * dd], w_l[:, 4 * dd + 2 * heads:]], axis=1).astype(BF16)
        w_ba = jnp.pad(w_l[:, 4 * dd:4 * dd + 2 * heads], ((0, 0), (0, LANE - 2 * heads))).astype(BF16)
        w_o = w_out[l, :dd].astype(BF16)
        w_u = w_out[l, dd:].astype(BF16)
        alog = _lane_row(a_log[l], heads)
        dtb = _lane_row(dt_bias[l], heads)
        hng = head_norm_g[l].reshape(1, dk)
        bdw, lng, lnb = b_dw[l].reshape(1, dc), ln_g[l].reshape(1, dc), ln_b[l].reshape(1, dc)
        final = l == depth - 1

        proj, ba = _inproj(hp, mod_p, norm_g[l], w_main, w_ba, bb=1, tt=tt_in, tn=tn)
        o, s_new, qbuf = _delta_prompt(proj, ba, w_qkv_conv[l], alog, dtb, hng,
                                       b=bp, t=tp, heads=heads, dk=dk)
        u, gbuf = _conf_prompt(proj, w_dw[l], bdw, lng, lnb, b=bp, t=tp, dd=dd, dc=dc, tt=tt_conf)
        hp = _outproj(hp, mod_p, o, u, w_o, w_u, final_g, bb=1, tt=tt_out, final=final)
        sp_l.append(s_new); qp_l.append(qbuf); gp_l.append(gbuf)

        proj, ba = _inproj(hs, mod_s, norm_g[l], w_main, w_ba, bb=bb_s, tt=ts, tn=tn)
        qhist = jnp.pad(state_qkv_conv[l], ((0, 0), (QKV_HIST - (QKV_CONV - 1), 0), (0, 0)))
        ghist = jnp.pad(state_glu_conv[l], ((0, 0), (GLU_HIST - (CONF_KERNEL - 1), 0), (0, 0)))
        o, s_new, qbuf = _delta_sample(proj, ba, qhist, state_delta, l, w_qkv_conv[l], alog, dtb, hng,
                                       b=bs, t=ts, heads=heads, dk=dk)
        u, gbuf = _conf_sample(proj, ghist, w_dw[l], bdw, lng, lnb, b=bs, t=ts, dd=dd, dc=dc,
                               bb=bb_conf)
        hs = _outproj(hs, mod_s, o, u, w_o, w_u, final_g, bb=bb_out, tt=ts, final=final)
        ss_l.append(s_new); qs_l.append(qbuf); gs_l.append(gbuf)

    return (hp, hs, jnp.stack(sp_l), jnp.stack(qp_l), jnp.stack(gp_l),
            jnp.stack(ss_l), jnp.stack(qs_l), jnp.stack(gs_l))
```

```python
import functools

import jax
import jax.numpy as jnp
from jax import lax
from jax.experimental import pallas as pl
from jax.experimental.pallas import tpu as pltpu

F32 = jnp.float32
BF16 = jnp.bfloat16

EPS = 1e-6
QKV_CONV = 4
CONF_KERNEL = 31
CHUNK = 64
LANE = 128
SUBLANE = 8
QKV_HIST = SUBLANE
GLU_HIST = 32
CONF_PARTIALS = 3
VMEM_LIMIT_BYTES = 56 * 1024 * 1024


def _params(n_axes):
    return pltpu.CompilerParams(dimension_semantics=("arbitrary",) * n_axes,
                                vmem_limit_bytes=VMEM_LIMIT_BYTES)


def _sigmoid(x):
    return 0.5 * jnp.tanh(0.5 * x) + 0.5


def _silu(x):
    return x * _sigmoid(x)


def _softplus(x):
    return jnp.maximum(x, 0.0) + jnp.log1p(jnp.exp(-jnp.abs(x)))


def _dot(a, b):
    return jnp.dot(a, b, preferred_element_type=F32)


def _dot_nt(a, b):
    return lax.dot_general(a, b, (((1,), (1,)), ((), ())), preferred_element_type=F32)


def _dot_tn(a, b):
    return lax.dot_general(a, b, (((0,), (0,)), ((), ())), preferred_element_type=F32)


def _ada_kernel(c_ref, w_ref, b_ref, o_ref):
    s = _silu(c_ref[...]).astype(BF16)
    o_ref[0] = _dot(s, w_ref[0].astype(BF16)) + b_ref[0]


def _ada(c_all, w_ada, b_ada):
    depth, d, n = w_ada.shape
    rows = c_all.shape[0]
    tn = min(n, 512)
    return pl.pallas_call(
        _ada_kernel,
        out_shape=jax.ShapeDtypeStruct((depth, rows, n), F32),
        grid=(depth, n // tn),
        in_specs=[pl.BlockSpec((rows, d), lambda l, j: (0, 0)),
                  pl.BlockSpec((1, d, tn), lambda l, j: (l, 0, j)),
                  pl.BlockSpec((1, 1, tn), lambda l, j: (l, 0, j))],
        out_specs=pl.BlockSpec((1, rows, tn), lambda l, j: (l, 0, j)),
        compiler_params=_params(2),
        name="ada_mod",
    )(c_all, w_ada, b_ada.reshape(depth, 1, n))


def _inproj_kernel(x_ref, shift_ref, scale_ref, g_ref, wa_ref, wb_ref, wba_ref, proj_ref, ba_ref, h_ref,
                   *, n_a):
    k = pl.program_id(2)

    @pl.when(k == 0)
    def _():
        x = x_ref[...]
        ms = jnp.mean(x * x, axis=-1, keepdims=True)
        h = x * lax.rsqrt(ms + EPS) * g_ref[0] * (1.0 + scale_ref[...]) + shift_ref[...]
        h = h.reshape(h_ref.shape).astype(BF16)
        h_ref[...] = h
        ba_ref[...] = _dot(h, wba_ref[0])

    @pl.when(k < n_a)
    def _():
        proj_ref[...] = _dot(h_ref[...], wa_ref[0])

    @pl.when(k >= n_a)
    def _():
        proj_ref[...] = _dot(h_ref[...], wb_ref[0])


def _inproj(x, mod, norm_g, w_a, w_b, w_ba, layer, *, bb, tt, tn):
    b, t, d = x.shape
    n_a, n_b = w_a.shape[2] // tn, w_b.shape[2] // tn
    n = w_a.shape[2] + w_b.shape[2]
    nb, nt, nn = b // bb, t // tt, n_a + n_b
    tm = bb * tt
    return pl.pallas_call(
        functools.partial(_inproj_kernel, n_a=n_a),
        out_shape=(jax.ShapeDtypeStruct((b * t, n), F32),
                   jax.ShapeDtypeStruct((b * t, LANE), F32)),
        grid=(nb, nt, nn),
        in_specs=[pl.BlockSpec((bb, tt, d), lambda i, j, k: (i, j, 0)),
                  pl.BlockSpec((bb, 1, d), lambda i, j, k: (i, 0, 0)),
                  pl.BlockSpec((bb, 1, d), lambda i, j, k: (i, 0, 1)),
                  pl.BlockSpec((1, 1, d), lambda i, j, k: (layer, 0, 0)),
                  pl.BlockSpec((1, d, tn), lambda i, j, k: (layer, 0, jnp.minimum(k, n_a - 1))),
                  pl.BlockSpec((1, d, tn), lambda i, j, k: (layer, 0, jnp.maximum(k - n_a, 0))),
                  pl.BlockSpec((1, d, LANE), lambda i, j, k: (layer, 0, 0))],
        out_specs=(pl.BlockSpec((tm, tn), lambda i, j, k: (i * nt + j, k)),
                   pl.BlockSpec((tm, LANE), lambda i, j, k: (i * nt + j, 0))),
        scratch_shapes=[pltpu.VMEM((tm, d), BF16)],
        compiler_params=_params(3),
        name="in_proj",
    )(x, mod, mod, norm_g, w_a, w_b, w_ba)


def _short_conv(xe, wconv_ref):
    acc = wconv_ref[0, QKV_CONV - 1:QKV_CONV, :] * xe
    for d in range(1, QKV_CONV):
        acc = acc + wconv_ref[0, QKV_CONV - 1 - d:QKV_CONV - d, :] * pltpu.roll(xe, d, axis=0)
    return acc


def _gates(ba, alog, dtb):
    beta = _sigmoid(ba)
    g = -jnp.exp(alog) * _softplus(ba + dtb)
    return beta, g


def _delta_rows(qkv, z_ref, ba, alog, dtb, hng, s_prev, *, c, nseq, heads, dk):
    rows_n = nseq * c
    dd = heads * dk
    hs = range(heads)
    rows = lax.broadcasted_iota(jnp.int32, (rows_n, rows_n), 0)
    cols = lax.broadcasted_iota(jnp.int32, (rows_n, rows_n), 1)
    if nseq > 1:
        shift = c.bit_length() - 1
        same = (rows >> shift) == (cols >> shift)
        causal = same & (rows >= cols)
        strict = same & (rows > cols)
        upper = same & (rows <= cols)
        ones = jnp.where(same, 1.0, 0.0).astype(BF16)
    else:
        causal = rows >= cols
        strict = rows > cols
        upper = rows <= cols
        ones = jnp.ones((rows_n, rows_n), BF16)
    lower_b = jnp.where(causal, 1.0, 0.0).astype(BF16)
    upper_b = jnp.where(upper, 1.0, 0.0).astype(BF16)
    eye = jnp.where(rows == cols, 1.0, 0.0)

    beta, g = _gates(ba, alog, dtb)
    g0 = g.astype(BF16)
    r1 = g - g0.astype(F32)
    g1 = r1.astype(BF16)
    g2 = (r1 - g1.astype(F32)).astype(BF16)
    gc = _dot(lower_b, g0) + (_dot(lower_b, g1) + _dot(lower_b, g2))
    gct = _dot_tn(g0, upper_b) + (_dot_tn(g1, upper_b) + _dot_tn(g2, upper_b))
    gl = _dot(ones, g0) + (_dot(ones, g1) + _dot(ones, g2))

    def l2n(x):
        return x * lax.rsqrt(jnp.sum(x * x, axis=-1, keepdims=True) + EPS)

    qn = [l2n(qkv[:, h * dk:(h + 1) * dk]) * (dk ** -0.5) for h in hs]
    kn = [l2n(qkv[:, dd + h * dk:dd + (h + 1) * dk]) for h in hs]
    vv = [qkv[:, 2 * dd + h * dk:2 * dd + (h + 1) * dk] for h in hs]
    bcol = [beta[:, h:h + 1] for h in hs]
    gcol = [gc[:, heads + h:heads + h + 1] for h in hs]
    glc = [gl[:, heads + h:heads + h + 1] for h in hs]
    decay = [jnp.where(causal, jnp.exp(jnp.where(causal, gcol[h] - gct[heads + h:heads + h + 1, :], 0.0)), 0.0)
             for h in hs]
    kb = [kn[h].astype(BF16) for h in hs]
    kq = [jnp.concatenate([kn[h], qn[h]], axis=0).astype(BF16) for h in hs]
    kkqk = [_dot_nt(kq[h], kb[h]) for h in hs]
    low = [jnp.where(strict, bcol[h] * kkqk[h][:rows_n] * decay[h], 0.0) for h in hs]
    attn = [(kkqk[h][rows_n:] * decay[h]).astype(BF16) for h in hs]
    inv = [eye - jnp.where((rows ^ cols) == 1, low[h], 0.0) for h in hs]
    b = 2
    while b < c:
        pair = ((rows ^ cols) >= b) & ((rows ^ cols) < 2 * b)
        off = [jnp.where(pair, low[h], 0.0).astype(BF16) for h in hs]
        invb = [inv[h].astype(BF16) for h in hs]
        y = [_dot(off[h], invb[h]).astype(BF16) for h in hs]
        inv = [inv[h] - _dot(invb[h], y[h]) for h in hs]
        b *= 2
    egc = [jnp.exp(gcol[h]) for h in hs]
    rhs = [jnp.concatenate([bcol[h] * vv[h], (bcol[h] * egc[h]) * kn[h]], axis=1).astype(BF16) for h in hs]
    sol = [_dot(inv[h].astype(BF16), rhs[h]) for h in hs]
    qg = [qn[h] * egc[h] for h in hs]
    kdb = [(kn[h] * jnp.exp(glc[h] - gcol[h])).astype(BF16) for h in hs]

    if nseq == 1:
        sb = [s_prev[0][h].astype(BF16) for h in hs]
        wq = [jnp.concatenate([sol[h][:, dk:], qg[h]], axis=0).astype(BF16) for h in hs]
        ys = [_dot(wq[h], sb[h]) for h in hs]
        ub = [(sol[h][:, :dk] - ys[h][:rows_n]).astype(BF16) for h in hs]
        o = [ys[h][rows_n:] + _dot(attn[h], ub[h]) for h in hs]
        s_new = [[jnp.exp(glc[h][0:1, :]) * s_prev[0][h] + _dot_tn(kdb[h], ub[h]) for h in hs]]
    else:
        js = range(nseq)
        sb = [[s_prev[j][h].astype(BF16) for h in hs] for j in js]
        wq = [[jnp.concatenate([sol[h][j * c:(j + 1) * c, dk:], qg[h][j * c:(j + 1) * c]], axis=0).astype(BF16)
               for h in hs] for j in js]
        ys = [[_dot(wq[j][h], sb[j][h]) for h in hs] for j in js]
        ws = [jnp.concatenate([ys[j][h][:c] for j in js], axis=0) for h in hs]
        qs = [jnp.concatenate([ys[j][h][c:] for j in js], axis=0) for h in hs]
        u = [sol[h][:, :dk] - ws[h] for h in hs]
        o = [qs[h] + _dot(attn[h], u[h].astype(BF16)) for h in hs]
        seq_of_row = lax.broadcasted_iota(jnp.int32, (rows_n, dk), 0) >> (c.bit_length() - 1)
        s_new = [[None] * heads for _ in js]
        for j in range(0, nseq, 2):
            um = [jnp.concatenate([jnp.where(seq_of_row == j, u[h], 0.0),
                                   jnp.where(seq_of_row == j + 1, u[h], 0.0)], axis=1).astype(BF16) for h in hs]
            sn = [_dot_tn(kdb[h], um[h]) for h in hs]
            for h in hs:
                s_new[j][h] = jnp.exp(glc[h][j * c:j * c + 1, :]) * s_prev[j][h] + sn[h][:, :dk]
                s_new[j + 1][h] = (jnp.exp(glc[h][(j + 1) * c:(j + 1) * c + 1, :]) * s_prev[j + 1][h]
                                   + sn[h][:, dk:])
    out = []
    for h in hs:
        on = o[h] * lax.rsqrt(jnp.mean(o[h] * o[h], axis=-1, keepdims=True) + EPS) * hng
        out.append(on * _silu(z_ref[:, h * dk:(h + 1) * dk]))
    return out, s_new


def _delta_prompt_kernel(qkv_ref, z_ref, ba_ref, wconv_ref, alog_ref, dtb_ref, hng_ref,
                         o_ref, s_ref, qbuf_ref, ext_ref, *, c, heads, dk):
    i = pl.program_id(1)

    @pl.when(i == 0)
    def _():
        ext_ref[...] = jnp.zeros(ext_ref.shape, F32)
        s_ref[...] = jnp.zeros(s_ref.shape, F32)

    xe = jnp.concatenate([ext_ref[...], qkv_ref[...]], axis=0)
    acc = _short_conv(xe, wconv_ref)[QKV_HIST:]
    ext_ref[...] = qkv_ref[c - QKV_HIST:c, :]

    s_prev = [[s_ref[0, h] for h in range(heads)]]
    out, s_new = _delta_rows(_silu(acc), z_ref, ba_ref[...], alog_ref[0], dtb_ref[0], hng_ref[0],
                             s_prev, c=c, nseq=1, heads=heads, dk=dk)
    for h in range(heads):
        o_ref[:, h * dk:(h + 1) * dk] = out[h].astype(o_ref.dtype)
        s_ref[0, h] = s_new[0][h]

    @pl.when(i == pl.num_programs(1) - 1)
    def _():
        qbuf_ref[0] = qkv_ref[c - (QKV_CONV - 1):c, :]


def _layer_spec(arr, layer):
    zeros = (0,) * (arr.ndim - 1)
    return pl.BlockSpec((1,) + arr.shape[1:], lambda *_: (layer,) + zeros)


def _delta_prompt(proj, ba, wconv, alog, dtb, hng, layer, *, b, t, heads, dk):
    dd = heads * dk
    c = min(CHUNK, t)
    nc = t // c
    kern = functools.partial(_delta_prompt_kernel, c=c, heads=heads, dk=dk)
    return pl.pallas_call(
        kern,
        out_shape=(jax.ShapeDtypeStruct((b * t, dd), BF16),
                   jax.ShapeDtypeStruct((b, heads, dk, dk), F32),
                   jax.ShapeDtypeStruct((b, QKV_CONV - 1, 3 * dd), F32)),
        grid=(b, nc),
        in_specs=[pl.BlockSpec((c, 3 * dd), lambda i, j: (i * nc + j, 0)),
                  pl.BlockSpec((c, dd), lambda i, j: (i * nc + j, 3)),
                  pl.BlockSpec((c, LANE), lambda i, j: (i * nc + j, 0)),
                  _layer_spec(wconv, layer), _layer_spec(alog, layer), _layer_spec(dtb, layer),
                  _layer_spec(hng, layer)],
        out_specs=(pl.BlockSpec((c, dd), lambda i, j: (i * nc + j, 0)),
                   pl.BlockSpec((1, heads, dk, dk), lambda i, j: (i, 0, 0, 0)),
                   pl.BlockSpec((1, QKV_CONV - 1, 3 * dd), lambda i, j: (i, 0, 0))),
        scratch_shapes=[pltpu.VMEM((QKV_HIST, 3 * dd), F32)],
        compiler_params=_params(2),
        name="delta_prompt",
    )(proj, proj, ba, wconv, alog, dtb, hng)


def _write_slab(ref, slab, value):
    for other in range(ref.shape[0]):
        if other != slab:
            ref[other] = jnp.zeros(ref.shape[1:], ref.dtype)
    ref[slab] = value


def _delta_sample_kernel(qkv_ref, z_ref, ba_ref, hist_ref, s0_ref, wconv_ref, alog_ref, dtb_ref,
                         hng_ref, *rest, t, nseq, heads, dk, slab):
    o_ref, s_ref, qbuf_ref, ext_ref = rest[-4:]
    n = ext_ref.shape[2]
    first = QKV_HIST - (QKV_CONV - 1)
    ext_ref[:, 0:first, :] = jnp.zeros((nseq, first, n), F32)
    ext_ref[:, first:QKV_HIST, :] = hist_ref[0]
    ext_ref[:, QKV_HIST:QKV_HIST + t, :] = qkv_ref[...].reshape(nseq, t, n)
    xe = ext_ref[...].reshape(nseq * (QKV_HIST + t), n)
    acc = _short_conv(xe, wconv_ref).reshape(nseq, QKV_HIST + t, n)[:, QKV_HIST:, :].reshape(nseq * t, n)
    _write_slab(qbuf_ref, slab, ext_ref[:, QKV_HIST + t - (QKV_CONV - 1):QKV_HIST + t, :])

    s_prev = [[s0_ref[0, j, h] for h in range(heads)] for j in range(nseq)]
    out, s_new = _delta_rows(_silu(acc), z_ref, ba_ref[...], alog_ref[0], dtb_ref[0], hng_ref[0],
                             s_prev, c=t, nseq=nseq, heads=heads, dk=dk)
    for other in range(s_ref.shape[0]):
        if other != slab:
            s_ref[other] = jnp.zeros(s_ref.shape[1:], F32)
    for h in range(heads):
        o_ref[:, h * dk:(h + 1) * dk] = out[h].astype(o_ref.dtype)
        for j in range(nseq):
            s_ref[slab, j, h] = s_new[j][h]


def _layered_out(depth, layer, prev, block_tail, n_in):
    zeros = (0,) * (len(block_tail) - 1)
    if prev is None:
        return pl.BlockSpec((depth,) + block_tail, lambda i: (0, i) + zeros), layer, None
    return pl.BlockSpec((1,) + block_tail, lambda i: (layer, i) + zeros), 0, n_in


def _delta_sample(proj, ba, hist, state, prev, layer, wconv, alog, dtb, hng, *, b, t, nseq, heads, dk):
    dd = heads * dk
    depth = state.shape[0]
    rows_n = nseq * t
    in_specs = [pl.BlockSpec((rows_n, 3 * dd), lambda i: (i, 0)),
                pl.BlockSpec((rows_n, dd), lambda i: (i, 3)),
                pl.BlockSpec((rows_n, LANE), lambda i: (i, 0)),
                pl.BlockSpec((1, nseq, QKV_CONV - 1, 3 * dd), lambda i: (layer, i, 0, 0)),
                pl.BlockSpec((1, nseq, heads, dk, dk), lambda i: (layer, i, 0, 0, 0)),
                _layer_spec(wconv, layer), _layer_spec(alog, layer), _layer_spec(dtb, layer),
                _layer_spec(hng, layer)]
    args = [proj, proj, ba, hist, state, wconv, alog, dtb, hng]
    s_spec, slab, alias_at = _layered_out(depth, layer, prev, (nseq, heads, dk, dk), len(args))
    q_spec, _, _ = _layered_out(depth, layer, prev, (nseq, QKV_CONV - 1, 3 * dd), len(args))
    aliases = {}
    if prev is not None:
        in_specs += [pl.BlockSpec(memory_space=pl.ANY)] * 2
        args += list(prev)
        aliases = {alias_at: 1, alias_at + 1: 2}
    kern = functools.partial(_delta_sample_kernel, t=t, nseq=nseq, heads=heads, dk=dk, slab=slab)
    return pl.pallas_call(
        kern,
        out_shape=(jax.ShapeDtypeStruct((b * t, dd), BF16),
                   jax.ShapeDtypeStruct((depth, b, heads, dk, dk), F32),
                   jax.ShapeDtypeStruct((depth, b, QKV_CONV - 1, 3 * dd), F32)),
        grid=(b // nseq,),
        in_specs=in_specs,
        out_specs=(pl.BlockSpec((rows_n, dd), lambda i: (i, 0)), s_spec, q_spec),
        scratch_shapes=[pltpu.VMEM((nseq, QKV_HIST + t, 3 * dd), F32)],
        input_output_aliases=aliases,
        compiler_params=_params(1),
        name="delta_sample",
    )(*args)


def _conf_epilogue(conv, zc, bdw, lng, lnb):
    y = conv + bdw
    mu = jnp.mean(y, axis=-1, keepdims=True)
    yc = y - mu
    var = jnp.mean(yc * yc, axis=-1, keepdims=True)
    y = yc * lax.rsqrt(var + EPS) * lng + lnb
    return _silu(y) * _silu(zc)


def _conf_prompt_kernel(ga_ref, gb_ref, zc_ref, w_ref, bdw_ref, lng_ref, lnb_ref,
                        u_ref, gbuf_ref, ext_ref, sh_ref, conv_ref, *, tt, rr):
    i = pl.program_id(1)
    dc = ext_ref.shape[1]
    first = GLU_HIST - (CONF_KERNEL - 1)

    @pl.when(i == 0)
    def _():
        ext_ref[0:GLU_HIST, :] = jnp.zeros((GLU_HIST, dc), F32)
        ext_ref[GLU_HIST + tt:GLU_HIST + tt + SUBLANE, :] = jnp.zeros((SUBLANE, dc), F32)

    ext_ref[GLU_HIST:GLU_HIST + tt, :] = ga_ref[...] * _sigmoid(gb_ref[...])
    for p in range(SUBLANE):
        sh_ref[p] = ext_ref[p:p + GLU_HIST + tt, :]

    for lt in range(dc // LANE):
        lanes = slice(lt * LANE, (lt + 1) * LANE)
        taps = [jnp.broadcast_to(w_ref[0, j:j + 1, lanes], (rr, LANE)) for j in range(CONF_KERNEL)]

        def body(r, carry, lanes=lanes, taps=taps):
            r0 = pl.multiple_of(r * rr, rr)
            parts = [None] * CONF_PARTIALS
            for j in range(CONF_KERNEL):
                off = first + j
                win = sh_ref[off % SUBLANE, pl.ds(r0 + (off // SUBLANE) * SUBLANE, rr), lanes]
                p = j % CONF_PARTIALS
                parts[p] = taps[j] * win if parts[p] is None else parts[p] + taps[j] * win
            acc = parts[0]
            for p in range(1, CONF_PARTIALS):
                acc = acc + parts[p]
            conv_ref[pl.ds(r0, rr), lanes] = acc
            return carry

        lax.fori_loop(0, tt // rr, body, 0)

    y = _conf_epilogue(conv_ref[...], zc_ref[...], bdw_ref[0], lng_ref[0], lnb_ref[0])
    u_ref[...] = y.astype(u_ref.dtype)
    ext_ref[0:GLU_HIST, :] = ext_ref[tt:tt + GLU_HIST, :]

    @pl.when(i == pl.num_programs(1) - 1)
    def _():
        gbuf_ref[0] = ext_ref[first:GLU_HIST, :]


def _conf_prompt(proj, w_dw, b_dw, ln_g, ln_b, layer, *, b, t, dd, dc, tt):
    nt = t // tt
    rr = min(32, tt)
    col0 = (4 * dd) // dc
    kern = functools.partial(_conf_prompt_kernel, tt=tt, rr=rr)
    row = lambda i, j: i * nt + j
    return pl.pallas_call(
        kern,
        out_shape=(jax.ShapeDtypeStruct((b * t, dc), BF16),
                   jax.ShapeDtypeStruct((b, CONF_KERNEL - 1, dc), F32)),
        grid=(b, nt),
        in_specs=[pl.BlockSpec((tt, dc), lambda i, j: (row(i, j), col0)),
                  pl.BlockSpec((tt, dc), lambda i, j: (row(i, j), col0 + 1)),
                  pl.BlockSpec((tt, dc), lambda i, j: (row(i, j), col0 + 2)),
                  _layer_spec(w_dw, layer), _layer_spec(b_dw, layer), _layer_spec(ln_g, layer),
                  _layer_spec(ln_b, layer)],
        out_specs=(pl.BlockSpec((tt, dc), lambda i, j: (row(i, j), 0)),
                   pl.BlockSpec((1, CONF_KERNEL - 1, dc), lambda i, j: (i, 0, 0))),
        scratch_shapes=[pltpu.VMEM((GLU_HIST + tt + SUBLANE, dc), F32),
                        pltpu.VMEM((SUBLANE, GLU_HIST + tt, dc), F32),
                        pltpu.VMEM((tt, dc), F32)],
        compiler_params=_params(2),
        name="conf_prompt",
    )(proj, proj, proj, w_dw, b_dw, ln_g, ln_b)


def _conf_sample_kernel(ga_ref, gb_ref, zc_ref, hist_ref, w_ref, bdw_ref, lng_ref, lnb_ref,
                        *rest, t, bb, slab):
    u_ref, gbuf_ref, ext_ref, sh_ref, conv_ref = rest[-5:]
    dc = ext_ref.shape[2]
    first = GLU_HIST - (CONF_KERNEL - 1)
    rows = GLU_HIST + t
    ext_ref[:, 0:first, :] = jnp.zeros((bb, first, dc), F32)
    ext_ref[:, first:GLU_HIST, :] = hist_ref[0]
    ext_ref[:, GLU_HIST:rows, :] = (ga_ref[...] * _sigmoid(gb_ref[...])).reshape(bb, t, dc)
    ext_ref[:, rows:rows + SUBLANE, :] = jnp.zeros((bb, SUBLANE, dc), F32)
    for p in range(SUBLANE):
        sh_ref[p] = ext_ref[:, p:p + rows, :]
    for lt in range(dc // LANE):
        lanes = slice(lt * LANE, (lt + 1) * LANE)
        acc = None
        for j in range(CONF_KERNEL):
            off = first + j
            base = (off // SUBLANE) * SUBLANE
            term = w_ref[0, j:j + 1, lanes] * sh_ref[off % SUBLANE, :, base:base + t, lanes]
            acc = term if acc is None else acc + term
        conv_ref[:, lanes] = acc.reshape(bb * t, LANE)
    y = _conf_epilogue(conv_ref[...], zc_ref[...], bdw_ref[0], lng_ref[0], lnb_ref[0])
    u_ref[...] = y.astype(u_ref.dtype)
    _write_slab(gbuf_ref, slab, ext_ref[:, first + t:GLU_HIST + t, :])


def _conf_sample(proj, hist, prev, layer, w_dw, b_dw, ln_g, ln_b, *, b, t, dd, dc, bb):
    depth = hist.shape[0]
    col0 = (4 * dd) // dc
    in_specs = [pl.BlockSpec((bb * t, dc), lambda i: (i, col0)),
                pl.BlockSpec((bb * t, dc), lambda i: (i, col0 + 1)),
                pl.BlockSpec((bb * t, dc), lambda i: (i, col0 + 2)),
                pl.BlockSpec((1, bb, CONF_KERNEL - 1, dc), lambda i: (layer, i, 0, 0)),
                _layer_spec(w_dw, layer), _layer_spec(b_dw, layer), _layer_spec(ln_g, layer),
                _layer_spec(ln_b, layer)]
    args = [proj, proj, proj, hist, w_dw, b_dw, ln_g, ln_b]
    g_spec, slab, alias_at = _layered_out(depth, layer, prev, (bb, CONF_KERNEL - 1, dc), len(args))
    aliases = {}
    if prev is not None:
        in_specs.append(pl.BlockSpec(memory_space=pl.ANY))
        args.append(prev)
        aliases = {alias_at: 1}
    kern = functools.partial(_conf_sample_kernel, t=t, bb=bb, slab=slab)
    return pl.pallas_call(
        kern,
        out_shape=(jax.ShapeDtypeStruct((b * t, dc), BF16),
                   jax.ShapeDtypeStruct((depth, b, CONF_KERNEL - 1, dc), F32)),
        grid=(b // bb,),
        in_specs=in_specs,
        out_specs=(pl.BlockSpec((bb * t, dc), lambda i: (i, 0)), g_spec),
        scratch_shapes=[pltpu.VMEM((bb, GLU_HIST + t + SUBLANE, dc), F32),
                        pltpu.VMEM((SUBLANE, bb, GLU_HIST + t, dc), F32),
                        pltpu.VMEM((bb * t, dc), F32)],
        input_output_aliases=aliases,
        compiler_params=_params(1),
        name="conf_sample",
    )(*args)


def _outproj_kernel(x_ref, gate_ref, o_ref, u_ref, wo_ref, wu_ref, fg_ref, y_ref, *, final):
    mix = _dot(o_ref[...], wo_ref[0]) + _dot(u_ref[...], wu_ref[0])
    x = x_ref[...]
    y = x + gate_ref[...] * mix.reshape(x.shape)
    if final:
        y = y * lax.rsqrt(jnp.mean(y * y, axis=-1, keepdims=True) + EPS) * fg_ref[...]
    y_ref[...] = y


def _outproj(x, mod, o, u, w_out, final_g, layer, *, bb, tt, final):
    b, t, d = x.shape
    dd, dc = o.shape[1], u.shape[1]
    nt = t // tt
    tm = bb * tt
    kern = functools.partial(_outproj_kernel, final=final)
    return pl.pallas_call(
        kern,
        out_shape=jax.ShapeDtypeStruct((b, t, d), F32),
        grid=(b // bb, nt),
        in_specs=[pl.BlockSpec((bb, tt, d), lambda i, j: (i, j, 0)),
                  pl.BlockSpec((bb, 1, d), lambda i, j: (i, 0, 2)),
                  pl.BlockSpec((tm, dd), lambda i, j: (i * nt + j, 0)),
                  pl.BlockSpec((tm, dc), lambda i, j: (i * nt + j, 0)),
                  pl.BlockSpec((1, dd, d), lambda i, j: (layer, 0, 0)),
                  pl.BlockSpec((1, dc, d), lambda i, j: (layer, dd // dc, 0)),
                  pl.BlockSpec((1, d), lambda i, j: (0, 0))],
        out_specs=pl.BlockSpec((bb, tt, d), lambda i, j: (i, j, 0)),
        compiler_params=_params(2),
        name="out_proj",
    )(x, mod, o, u, w_out, w_out, final_g.reshape(1, d))


def kernel(x_prompt, x_sample, c_prompt, c_sample, state_delta, state_qkv_conv, state_glu_conv,
           norm_g, w_ada, b_ada, w_in, w_qkv_conv, a_log, dt_bias, head_norm_g, w_dw, b_dw,
           ln_g, ln_b, w_out, final_g):
    depth = w_in.shape[0]
    bp, tp, d = x_prompt.shape
    bs, ts, _ = x_sample.shape
    heads = a_log.shape[1]
    dd = w_qkv_conv.shape[2] // 3
    dk = dd // heads
    dc = w_dw.shape[2]
    assert ts % SUBLANE == 0 and ts >= QKV_CONV - 1 and ts <= CONF_KERNEL - 1
    assert 2 * heads <= LANE and dd % dc == 0 and tp % CHUNK == 0
    assert CHUNK % ts == 0 and bs % (CHUNK // ts) == 0 and (CHUNK // ts) % 2 == 0

    mod = _ada(jnp.concatenate([c_prompt, c_sample], axis=0), w_ada, b_ada)

    tn = min(1024, dd, dc)
    tm = 1024
    tt_in = min(tm, tp)
    bb_s = min(bs, tm // ts)
    tt_out = min(512, tp)
    bb_out = min(bs, 512 // ts)
    tt_conf = min(256, tp)
    bb_conf = min(bs, 8)
    nseq = CHUNK // ts

    w_a = w_in[:, :, :4 * dd].astype(BF16)
    w_b = w_in[:, :, 4 * dd + 2 * heads:].astype(BF16)
    w_ba = jnp.pad(w_in[:, :, 4 * dd:4 * dd + 2 * heads],
                   ((0, 0), (0, 0), (0, LANE - 2 * heads))).astype(BF16)
    w_out_b = w_out.astype(BF16)
    norm_g3 = norm_g.reshape(depth, 1, d)
    alog = jnp.pad(a_log, ((0, 0), (heads, LANE - 2 * heads))).reshape(depth, 1, LANE)
    dtb = jnp.pad(dt_bias, ((0, 0), (heads, LANE - 2 * heads))).reshape(depth, 1, LANE)
    hng = head_norm_g.reshape(depth, 1, dk)
    bdw, lng, lnb = b_dw.reshape(depth, 1, dc), ln_g.reshape(depth, 1, dc), ln_b.reshape(depth, 1, dc)

    hp, hs = x_prompt, x_sample
    sp_l, qp_l, gp_l = [], [], []
    s_s = q_s = g_s = None
    for l in range(depth):
        mod_p = mod[l, :bp].reshape(bp, 1, 3 * d)
        mod_s = mod[l, bp:].reshape(bs, 1, 3 * d)
        final = l == depth - 1

        proj, ba = _inproj(hp, mod_p, norm_g3, w_a, w_b, w_ba, l, bb=1, tt=tt_in, tn=tn)
        o, s_new, qbuf = _delta_prompt(proj, ba, w_qkv_conv, alog, dtb, hng, l,
                                       b=bp, t=tp, heads=heads, dk=dk)
        u, gbuf = _conf_prompt(proj, w_dw, bdw, lng, lnb, l, b=bp, t=tp, dd=dd, dc=dc, tt=tt_conf)
        hp = _outproj(hp, mod_p, o, u, w_out_b, final_g, l, bb=1, tt=tt_out, final=final)
        sp_l.append(s_new); qp_l.append(qbuf); gp_l.append(gbuf)

        proj, ba = _inproj(hs, mod_s, norm_g3, w_a, w_b, w_ba, l, bb=bb_s, tt=ts, tn=tn)
        o, s_s, q_s = _delta_sample(proj, ba, state_qkv_conv, state_delta,
                                    None if l == 0 else (s_s, q_s), l, w_qkv_conv, alog, dtb, hng,
                                    b=bs, t=ts, nseq=nseq, heads=heads, dk=dk)
        u, g_s = _conf_sample(proj, state_glu_conv, g_s, l, w_dw, bdw, lng, lnb,
                              b=bs, t=ts, dd=dd, dc=dc, bb=bb_conf)
        hs = _outproj(hs, mod_s, o, u, w_out_b, final_g, l, bb=bb_out, tt=ts, final=final)

    return (hp, hs, jnp.stack(sp_l), jnp.stack(qp_l), jnp.stack(gp_l), s_s, q_s, g_s)
```

```python
import functools

import jax
import jax.numpy as jnp
from jax import lax
from jax.experimental import pallas as pl
from jax.experimental.pallas import tpu as pltpu

F32 = jnp.float32
BF16 = jnp.bfloat16

EPS = 1e-6
QKV_CONV = 4
CONF_KERNEL = 31
CHUNK = 64
LANE = 128
SUBLANE = 8
QKV_HIST = SUBLANE
GLU_HIST = 32
CONF_PARTIALS = 3
VMEM_LIMIT_BYTES = 56 * 1024 * 1024


def _params(n_axes):
    return pltpu.CompilerParams(dimension_semantics=("arbitrary",) * n_axes,
                                vmem_limit_bytes=VMEM_LIMIT_BYTES)


def _sigmoid(x):
    return 0.5 * jnp.tanh(0.5 * x) + 0.5


def _silu(x):
    return x * _sigmoid(x)


def _softplus(x):
    return jnp.maximum(x, 0.0) + jnp.log1p(jnp.exp(-jnp.abs(x)))


def _dot(a, b):
    return jnp.dot(a, b, preferred_element_type=F32)


def _dot_nt(a, b):
    return lax.dot_general(a, b, (((1,), (1,)), ((), ())), preferred_element_type=F32)


def _dot_tn(a, b):
    return lax.dot_general(a, b, (((0,), (0,)), ((), ())), preferred_element_type=F32)


def _ada_kernel(c_ref, w_ref, b_ref, o_ref):
    s = _silu(c_ref[...]).astype(BF16)
    o_ref[0] = _dot(s, w_ref[0].astype(BF16)) + b_ref[0]


def _ada(c_all, w_ada, b_ada):
    depth, d, n = w_ada.shape
    rows = c_all.shape[0]
    tn = min(n, 512)
    return pl.pallas_call(
        _ada_kernel,
        out_shape=jax.ShapeDtypeStruct((depth, rows, n), F32),
        grid=(depth, n // tn),
        in_specs=[pl.BlockSpec((rows, d), lambda l, j: (0, 0)),
                  pl.BlockSpec((1, d, tn), lambda l, j: (l, 0, j)),
                  pl.BlockSpec((1, 1, tn), lambda l, j: (l, 0, j))],
        out_specs=pl.BlockSpec((1, rows, tn), lambda l, j: (l, 0, j)),
        compiler_params=_params(2),
        name="ada_mod",
    )(c_all, w_ada, b_ada.reshape(depth, 1, n))


def _inproj_kernel(x_ref, shift_ref, scale_ref, g_ref, wa_ref, wb_ref, wba_ref, proj_ref, ba_ref, h_ref,
                   *, n_a):
    k = pl.program_id(2)

    @pl.when(k == 0)
    def _():
        x = x_ref[...]
        ms = jnp.mean(x * x, axis=-1, keepdims=True)
        h = x * lax.rsqrt(ms + EPS) * g_ref[0] * (1.0 + scale_ref[...]) + shift_ref[...]
        h = h.reshape(h_ref.shape).astype(BF16)
        h_ref[...] = h
        ba_ref[...] = _dot_nt(h, wba_ref[0])

    @pl.when(k < n_a)
    def _():
        proj_ref[...] = _dot_nt(h_ref[...], wa_ref[0])

    @pl.when(k >= n_a)
    def _():
        proj_ref[...] = _dot_nt(h_ref[...], wb_ref[0])


def _inproj(x, mod, norm_g, w_a, w_b, w_ba, layer, *, bb, tt, tn):
    b, t, d = x.shape
    n_a, n_b = w_a.shape[1] // tn, w_b.shape[1] // tn
    n = w_a.shape[1] + w_b.shape[1]
    nb, nt, nn = b // bb, t // tt, n_a + n_b
    tm = bb * tt
    return pl.pallas_call(
        functools.partial(_inproj_kernel, n_a=n_a),
        out_shape=(jax.ShapeDtypeStruct((b * t, n), F32),
                   jax.ShapeDtypeStruct((b * t, LANE), F32)),
        grid=(nb, nt, nn),
        in_specs=[pl.BlockSpec((bb, tt, d), lambda i, j, k: (i, j, 0)),
                  pl.BlockSpec((bb, 1, d), lambda i, j, k: (i, 0, 0)),
                  pl.BlockSpec((bb, 1, d), lambda i, j, k: (i, 0, 1)),
                  pl.BlockSpec((1, 1, d), lambda i, j, k: (layer, 0, 0)),
                  pl.BlockSpec((1, tn, d), lambda i, j, k: (layer, jnp.minimum(k, n_a - 1), 0)),
                  pl.BlockSpec((1, tn, d), lambda i, j, k: (layer, jnp.maximum(k - n_a, 0), 0)),
                  pl.BlockSpec((1, LANE, d), lambda i, j, k: (layer, 0, 0))],
        out_specs=(pl.BlockSpec((tm, tn), lambda i, j, k: (i * nt + j, k)),
                   pl.BlockSpec((tm, LANE), lambda i, j, k: (i * nt + j, 0))),
        scratch_shapes=[pltpu.VMEM((tm, d), BF16)],
        compiler_params=_params(3),
        name="in_proj",
    )(x, mod, mod, norm_g, w_a, w_b, w_ba)


def _short_conv(tiles, pick_cur, pick_prev, wconv_ref):
    cur = pick_cur(tiles)
    sub = lax.broadcasted_iota(jnp.int32, cur.shape, cur.ndim - 2)
    acc = wconv_ref[0, QKV_CONV - 1:QKV_CONV, :] * cur
    for d in range(1, QKV_CONV):
        rot = pltpu.roll(tiles, d, axis=1)
        delayed = jnp.where(sub >= d, pick_cur(rot), pick_prev(rot))
        acc = acc + wconv_ref[0, QKV_CONV - 1 - d:QKV_CONV - d, :] * delayed
    return acc


def _gates(ba, alog, dtb):
    beta = _sigmoid(ba)
    g = -jnp.exp(alog) * _softplus(ba + dtb)
    return beta, g


def _delta_rows(qkv, z_ref, ba, alog, dtb, hng, s_prev, *, c, nseq, heads, dk):
    rows_n = nseq * c
    dd = heads * dk
    hs = range(heads)
    rows = lax.broadcasted_iota(jnp.int32, (rows_n, rows_n), 0)
    cols = lax.broadcasted_iota(jnp.int32, (rows_n, rows_n), 1)
    if nseq > 1:
        shift = c.bit_length() - 1
        same = (rows >> shift) == (cols >> shift)
        causal = same & (rows >= cols)
        strict = same & (rows > cols)
        upper = same & (rows <= cols)
        ones = jnp.where(same, 1.0, 0.0).astype(BF16)
    else:
        causal = rows >= cols
        strict = rows > cols
        upper = rows <= cols
        ones = jnp.ones((rows_n, rows_n), BF16)
    lower_b = jnp.where(causal, 1.0, 0.0).astype(BF16)
    upper_b = jnp.where(upper, 1.0, 0.0).astype(BF16)
    eye = jnp.where(rows == cols, 1.0, 0.0)

    beta, g = _gates(ba, alog, dtb)
    g0 = g.astype(BF16)
    r1 = g - g0.astype(F32)
    g1 = r1.astype(BF16)
    g2 = (r1 - g1.astype(F32)).astype(BF16)
    gc = _dot(lower_b, g0) + (_dot(lower_b, g1) + _dot(lower_b, g2))
    gct = _dot_tn(g0, upper_b) + (_dot_tn(g1, upper_b) + _dot_tn(g2, upper_b))
    gl = _dot(ones, g0) + (_dot(ones, g1) + _dot(ones, g2))
    yield

    def l2n(x):
        return x * lax.rsqrt(jnp.sum(x * x, axis=-1, keepdims=True) + EPS)

    qn = [l2n(qkv[:, h * dk:(h + 1) * dk]) * (dk ** -0.5) for h in hs]
    kn = [l2n(qkv[:, dd + h * dk:dd + (h + 1) * dk]) for h in hs]
    vv = [qkv[:, 2 * dd + h * dk:2 * dd + (h + 1) * dk] for h in hs]
    bcol = [beta[:, h:h + 1] for h in hs]
    gcol = [gc[:, heads + h:heads + h + 1] for h in hs]
    glc = [gl[:, heads + h:heads + h + 1] for h in hs]
    decay = [jnp.where(causal, jnp.exp(jnp.where(causal, gcol[h] - gct[heads + h:heads + h + 1, :], 0.0)), 0.0)
             for h in hs]
    kb = [kn[h].astype(BF16) for h in hs]
    kq = [jnp.concatenate([kn[h], qn[h]], axis=0).astype(BF16) for h in hs]
    kkqk = [_dot_nt(kq[h], kb[h]) for h in hs]
    yield
    low = [jnp.where(strict, bcol[h] * kkqk[h][:rows_n] * decay[h], 0.0) for h in hs]
    attn = [(kkqk[h][rows_n:] * decay[h]).astype(BF16) for h in hs]
    inv = [eye - jnp.where((rows ^ cols) == 1, low[h], 0.0) for h in hs]
    b = 2
    while b < c:
        pair = ((rows ^ cols) >= b) & ((rows ^ cols) < 2 * b)
        off = [jnp.where(pair, low[h], 0.0).astype(BF16) for h in hs]
        invb = [inv[h].astype(BF16) for h in hs]
        y = [_dot(off[h], invb[h]).astype(BF16) for h in hs]
        yield
        inv = [inv[h] - _dot(invb[h], y[h]) for h in hs]
        yield
        b *= 2
    egc = [jnp.exp(gcol[h]) for h in hs]
    rhs = [jnp.concatenate([bcol[h] * vv[h], (bcol[h] * egc[h]) * kn[h]], axis=1).astype(BF16) for h in hs]
    sol = [_dot(inv[h].astype(BF16), rhs[h]) for h in hs]
    yield
    qg = [qn[h] * egc[h] for h in hs]
    kdb = [(kn[h] * jnp.exp(glc[h] - gcol[h])).astype(BF16) for h in hs]

    if nseq == 1:
        sb = [s_prev[0][h].astype(BF16) for h in hs]
        wq = [jnp.concatenate([sol[h][:, dk:], qg[h]], axis=0).astype(BF16) for h in hs]
        ys = [_dot(wq[h], sb[h]) for h in hs]
        yield
        ub = [(sol[h][:, :dk] - ys[h][:rows_n]).astype(BF16) for h in hs]
        o = [ys[h][rows_n:] + _dot(attn[h], ub[h]) for h in hs]
        yield
        s_new = [[jnp.exp(glc[h][0:1, :]) * s_prev[0][h] + _dot_tn(kdb[h], ub[h]) for h in hs]]
    else:
        js = range(nseq)
        sb = [[s_prev[j][h].astype(BF16) for h in hs] for j in js]
        wq = [[jnp.concatenate([sol[h][j * c:(j + 1) * c, dk:], qg[h][j * c:(j + 1) * c]], axis=0).astype(BF16)
               for h in hs] for j in js]
        ys = [[_dot(wq[j][h], sb[j][h]) for h in hs] for j in js]
        ws = [jnp.concatenate([ys[j][h][:c] for j in js], axis=0) for h in hs]
        qs = [jnp.concatenate([ys[j][h][c:] for j in js], axis=0) for h in hs]
        u = [sol[h][:, :dk] - ws[h] for h in hs]
        o = [qs[h] + _dot(attn[h], u[h].astype(BF16)) for h in hs]
        seq_of_row = lax.broadcasted_iota(jnp.int32, (rows_n, dk), 0) >> (c.bit_length() - 1)
        s_new = [[None] * heads for _ in js]
        for j in range(0, nseq, 2):
            um = [jnp.concatenate([jnp.where(seq_of_row == j, u[h], 0.0),
                                   jnp.where(seq_of_row == j + 1, u[h], 0.0)], axis=1).astype(BF16) for h in hs]
            sn = [_dot_tn(kdb[h], um[h]) for h in hs]
            for h in hs:
                s_new[j][h] = jnp.exp(glc[h][j * c:j * c + 1, :]) * s_prev[j][h] + sn[h][:, :dk]
                s_new[j + 1][h] = (jnp.exp(glc[h][(j + 1) * c:(j + 1) * c + 1, :]) * s_prev[j + 1][h]
                                   + sn[h][:, dk:])
    out = []
    for h in hs:
        on = o[h] * lax.rsqrt(jnp.mean(o[h] * o[h], axis=-1, keepdims=True) + EPS) * hng
        out.append(on * _silu(z_ref[:, h * dk:(h + 1) * dk]))
    return out, s_new


def _lockstep(gens):
    results = [None] * len(gens)
    live = list(range(len(gens)))
    while live:
        for idx in list(live):
            try:
                next(gens[idx])
            except StopIteration as stop:
                results[idx] = stop.value
                live.remove(idx)
    return results


def _delta_prompt_kernel(qkv_ref, z_ref, ba_ref, wconv_ref, alog_ref, dtb_ref, hng_ref,
                         o_ref, s_ref, qbuf_ref, ext_ref, *, c, nb, heads, dk):
    i = pl.program_id(1)

    @pl.when(i == 0)
    def _():
        ext_ref[...] = jnp.zeros(ext_ref.shape, F32)
        s_ref[...] = jnp.zeros(s_ref.shape, F32)

    gens = []
    for g in range(nb):
        xe = jnp.concatenate([ext_ref[g], qkv_ref[g]], axis=0)
        tiles = xe.reshape((QKV_HIST + c) // SUBLANE, SUBLANE, xe.shape[1])
        acc = _short_conv(tiles, lambda a: a[1:], lambda a: a[:-1], wconv_ref).reshape(c, xe.shape[1])
        ext_ref[g] = qkv_ref[g, c - QKV_HIST:c, :]
        s_prev = [[s_ref[g, h] for h in range(heads)]]
        gens.append(_delta_rows(_silu(acc), z_ref.at[g], ba_ref[g], alog_ref[0], dtb_ref[0], hng_ref[0],
                                s_prev, c=c, nseq=1, heads=heads, dk=dk))
    for g, (out, s_new) in enumerate(_lockstep(gens)):
        for h in range(heads):
            o_ref[g, :, h * dk:(h + 1) * dk] = out[h].astype(o_ref.dtype)
            s_ref[g, h] = s_new[0][h]

    @pl.when(i == pl.num_programs(1) - 1)
    def _():
        qbuf_ref[...] = qkv_ref[:, c - (QKV_CONV - 1):c, :]


def _layer_spec(arr, layer):
    zeros = (0,) * (arr.ndim - 1)
    return pl.BlockSpec((1,) + arr.shape[1:], lambda *_: (layer,) + zeros)


def _delta_prompt(proj, ba, wconv, alog, dtb, hng, layer, *, b, t, nb, heads, dk):
    dd = heads * dk
    c = min(CHUNK, t)
    nc = t // c
    proj3 = proj.reshape(b, t, proj.shape[1])
    kern = functools.partial(_delta_prompt_kernel, c=c, nb=nb, heads=heads, dk=dk)
    o, s_new, qbuf = pl.pallas_call(
        kern,
        out_shape=(jax.ShapeDtypeStruct((b, t, dd), BF16),
                   jax.ShapeDtypeStruct((b, heads, dk, dk), F32),
                   jax.ShapeDtypeStruct((b, QKV_CONV - 1, 3 * dd), F32)),
        grid=(b // nb, nc),
        in_specs=[pl.BlockSpec((nb, c, 3 * dd), lambda i, j: (i, j, 0)),
                  pl.BlockSpec((nb, c, dd), lambda i, j: (i, j, 3)),
                  pl.BlockSpec((nb, c, LANE), lambda i, j: (i, j, 0)),
                  _layer_spec(wconv, layer), _layer_spec(alog, layer), _layer_spec(dtb, layer),
                  _layer_spec(hng, layer)],
        out_specs=(pl.BlockSpec((nb, c, dd), lambda i, j: (i, j, 0)),
                   pl.BlockSpec((nb, heads, dk, dk), lambda i, j: (i, 0, 0, 0)),
                   pl.BlockSpec((nb, QKV_CONV - 1, 3 * dd), lambda i, j: (i, 0, 0))),
        scratch_shapes=[pltpu.VMEM((nb, QKV_HIST, 3 * dd), F32)],
        compiler_params=_params(2),
        name="delta_prompt",
    )(proj3, proj3, ba.reshape(b, t, LANE), wconv, alog, dtb, hng)
    return o.reshape(b * t, dd), s_new, qbuf


def _write_slab(ref, slab, value):
    for other in range(ref.shape[0]):
        if other != slab:
            ref[other] = jnp.zeros(ref.shape[1:], ref.dtype)
    ref[slab] = value


def _delta_sample_kernel(qkv_ref, z_ref, ba_ref, hist_ref, s0_ref, wconv_ref, alog_ref, dtb_ref,
                         hng_ref, *rest, t, nseq, heads, dk, slab):
    o_ref, s_ref, qbuf_ref, ext_ref = rest[-4:]
    n = ext_ref.shape[2]
    first = QKV_HIST - (QKV_CONV - 1)
    ext_ref[:, 0:first, :] = jnp.zeros((nseq, first, n), F32)
    ext_ref[:, first:QKV_HIST, :] = hist_ref[0]
    ext_ref[:, QKV_HIST:QKV_HIST + t, :] = qkv_ref[...].reshape(nseq, t, n)
    per_seq = (QKV_HIST + t) // SUBLANE
    tiles = ext_ref[...].reshape(nseq * per_seq, SUBLANE, n)

    def pick(a, lo, hi):
        return a.reshape(nseq, per_seq, SUBLANE, n)[:, lo:hi].reshape(nseq * (per_seq - 1), SUBLANE, n)

    acc = _short_conv(tiles, lambda a: pick(a, 1, per_seq), lambda a: pick(a, 0, per_seq - 1),
                      wconv_ref).reshape(nseq * t, n)
    _write_slab(qbuf_ref, slab, ext_ref[:, QKV_HIST + t - (QKV_CONV - 1):QKV_HIST + t, :])

    s_prev = [[s0_ref[0, j, h] for h in range(heads)] for j in range(nseq)]
    (out, s_new), = _lockstep([_delta_rows(_silu(acc), z_ref, ba_ref[...], alog_ref[0], dtb_ref[0],
                                           hng_ref[0], s_prev, c=t, nseq=nseq, heads=heads, dk=dk)])
    for other in range(s_ref.shape[0]):
        if other != slab:
            s_ref[other] = jnp.zeros(s_ref.shape[1:], F32)
    for h in range(heads):
        o_ref[:, h * dk:(h + 1) * dk] = out[h].astype(o_ref.dtype)
        for j in range(nseq):
            s_ref[slab, j, h] = s_new[j][h]


def _layered_out(depth, layer, prev, block_tail, n_in):
    zeros = (0,) * (len(block_tail) - 1)
    if prev is None:
        return pl.BlockSpec((depth,) + block_tail, lambda i: (0, i) + zeros), layer, None
    return pl.BlockSpec((1,) + block_tail, lambda i: (layer, i) + zeros), 0, n_in


def _delta_sample(proj, ba, hist, state, prev, layer, wconv, alog, dtb, hng, *, b, t, nseq, heads, dk):
    dd = heads * dk
    depth = state.shape[0]
    rows_n = nseq * t
    in_specs = [pl.BlockSpec((rows_n, 3 * dd), lambda i: (i, 0)),
                pl.BlockSpec((rows_n, dd), lambda i: (i, 3)),
                pl.BlockSpec((rows_n, LANE), lambda i: (i, 0)),
                pl.BlockSpec((1, nseq, QKV_CONV - 1, 3 * dd), lambda i: (layer, i, 0, 0)),
                pl.BlockSpec((1, nseq, heads, dk, dk), lambda i: (layer, i, 0, 0, 0)),
                _layer_spec(wconv, layer), _layer_spec(alog, layer), _layer_spec(dtb, layer),
                _layer_spec(hng, layer)]
    args = [proj, proj, ba, hist, state, wconv, alog, dtb, hng]
    s_spec, slab, alias_at = _layered_out(depth, layer, prev, (nseq, heads, dk, dk), len(args))
    q_spec, _, _ = _layered_out(depth, layer, prev, (nseq, QKV_CONV - 1, 3 * dd), len(args))
    aliases = {}
    if prev is not None:
        in_specs += [pl.BlockSpec(memory_space=pl.ANY)] * 2
        args += list(prev)
        aliases = {alias_at: 1, alias_at + 1: 2}
    kern = functools.partial(_delta_sample_kernel, t=t, nseq=nseq, heads=heads, dk=dk, slab=slab)
    return pl.pallas_call(
        kern,
        out_shape=(jax.ShapeDtypeStruct((b * t, dd), BF16),
                   jax.ShapeDtypeStruct((depth, b, heads, dk, dk), F32),
                   jax.ShapeDtypeStruct((depth, b, QKV_CONV - 1, 3 * dd), F32)),
        grid=(b // nseq,),
        in_specs=in_specs,
        out_specs=(pl.BlockSpec((rows_n, dd), lambda i: (i, 0)), s_spec, q_spec),
        scratch_shapes=[pltpu.VMEM((nseq, QKV_HIST + t, 3 * dd), F32)],
        input_output_aliases=aliases,
        compiler_params=_params(1),
        name="delta_sample",
    )(*args)


def _conf_epilogue(conv, zc, bdw, lng, lnb):
    y = conv + bdw
    mu = jnp.mean(y, axis=-1, keepdims=True)
    yc = y - mu
    var = jnp.mean(yc * yc, axis=-1, keepdims=True)
    y = yc * lax.rsqrt(var + EPS) * lng + lnb
    return _silu(y) * _silu(zc)


def _conf_prompt_kernel(ga_ref, gb_ref, zc_ref, w_ref, bdw_ref, lng_ref, lnb_ref,
                        u_ref, gbuf_ref, ext_ref, sh_ref, conv_ref, *, tt, rr):
    i = pl.program_id(1)
    dc = ext_ref.shape[1]
    first = GLU_HIST - (CONF_KERNEL - 1)

    @pl.when(i == 0)
    def _():
        ext_ref[0:GLU_HIST, :] = jnp.zeros((GLU_HIST, dc), F32)
        ext_ref[GLU_HIST + tt:GLU_HIST + tt + SUBLANE, :] = jnp.zeros((SUBLANE, dc), F32)

    ext_ref[GLU_HIST:GLU_HIST + tt, :] = ga_ref[...] * _sigmoid(gb_ref[...])
    for p in range(SUBLANE):
        for lt in range(dc // LANE):
            sh_ref[p, lt] = ext_ref[p:p + GLU_HIST + tt, lt * LANE:(lt + 1) * LANE]

    for lt in range(dc // LANE):
        lanes = slice(lt * LANE, (lt + 1) * LANE)
        taps = [jnp.broadcast_to(w_ref[0, j:j + 1, lanes], (rr, LANE)) for j in range(CONF_KERNEL)]

        def body(r, carry, lt=lt, lanes=lanes, taps=taps):
            r0 = pl.multiple_of(r * rr, rr)
            parts = [None] * CONF_PARTIALS
            for j in range(CONF_KERNEL):
                off = first + j
                win = sh_ref[off % SUBLANE, lt, pl.ds(r0 + (off // SUBLANE) * SUBLANE, rr), :]
                p = j % CONF_PARTIALS
                parts[p] = taps[j] * win if parts[p] is None else parts[p] + taps[j] * win
            acc = parts[0]
            for p in range(1, CONF_PARTIALS):
                acc = acc + parts[p]
            conv_ref[pl.ds(r0, rr), lanes] = acc
            return carry

        lax.fori_loop(0, tt // rr, body, 0)

    y = _conf_epilogue(conv_ref[...], zc_ref[...], bdw_ref[0], lng_ref[0], lnb_ref[0])
    u_ref[...] = y.astype(u_ref.dtype)
    ext_ref[0:GLU_HIST, :] = ext_ref[tt:tt + GLU_HIST, :]

    @pl.when(i == pl.num_programs(1) - 1)
    def _():
        gbuf_ref[0] = ext_ref[first:GLU_HIST, :]


def _conf_prompt(proj, w_dw, b_dw, ln_g, ln_b, layer, *, b, t, dd, dc, tt):
    nt = t // tt
    rr = min(32, tt)
    col0 = (4 * dd) // dc
    kern = functools.partial(_conf_prompt_kernel, tt=tt, rr=rr)
    row = lambda i, j: i * nt + j
    return pl.pallas_call(
        kern,
        out_shape=(jax.ShapeDtypeStruct((b * t, dc), BF16),
                   jax.ShapeDtypeStruct((b, CONF_KERNEL - 1, dc), F32)),
        grid=(b, nt),
        in_specs=[pl.BlockSpec((tt, dc), lambda i, j: (row(i, j), col0)),
                  pl.BlockSpec((tt, dc), lambda i, j: (row(i, j), col0 + 1)),
                  pl.BlockSpec((tt, dc), lambda i, j: (row(i, j), col0 + 2)),
                  _layer_spec(w_dw, layer), _layer_spec(b_dw, layer), _layer_spec(ln_g, layer),
                  _layer_spec(ln_b, layer)],
        out_specs=(pl.BlockSpec((tt, dc), lambda i, j: (row(i, j), 0)),
                   pl.BlockSpec((1, CONF_KERNEL - 1, dc), lambda i, j: (i, 0, 0))),
        scratch_shapes=[pltpu.VMEM((GLU_HIST + tt + SUBLANE, dc), F32),
                        pltpu.VMEM((SUBLANE, dc // LANE, GLU_HIST + tt, LANE), F32),
                        pltpu.VMEM((tt, dc), F32)],
        compiler_params=_params(2),
        name="conf_prompt",
    )(proj, proj, proj, w_dw, b_dw, ln_g, ln_b)


def _conf_sample_kernel(ga_ref, gb_ref, zc_ref, hist_ref, w_ref, bdw_ref, lng_ref, lnb_ref,
                        *rest, t, bb, slab):
    u_ref, gbuf_ref, ext_ref, sh_ref, conv_ref = rest[-5:]
    dc = ext_ref.shape[2]
    first = GLU_HIST - (CONF_KERNEL - 1)
    rows = GLU_HIST + t
    ext_ref[:, 0:first, :] = jnp.zeros((bb, first, dc), F32)
    ext_ref[:, first:GLU_HIST, :] = hist_ref[0]
    ext_ref[:, GLU_HIST:rows, :] = (ga_ref[...] * _sigmoid(gb_ref[...])).reshape(bb, t, dc)
    ext_ref[:, rows:rows + SUBLANE, :] = jnp.zeros((bb, SUBLANE, dc), F32)
    for p in range(SUBLANE):
        for lt in range(dc // LANE):
            sh_ref[p, lt] = ext_ref[:, p:p + rows, lt * LANE:(lt + 1) * LANE]
    for lt in range(dc // LANE):
        lanes = slice(lt * LANE, (lt + 1) * LANE)
        acc = None
        for j in range(CONF_KERNEL):
            off = first + j
            base = (off // SUBLANE) * SUBLANE
            term = w_ref[0, j:j + 1, lanes] * sh_ref[off % SUBLANE, lt, :, base:base + t, :]
            acc = term if acc is None else acc + term
        conv_ref[:, lanes] = acc.reshape(bb * t, LANE)
    y = _conf_epilogue(conv_ref[...], zc_ref[...], bdw_ref[0], lng_ref[0], lnb_ref[0])
    u_ref[...] = y.astype(u_ref.dtype)
    _write_slab(gbuf_ref, slab, ext_ref[:, first + t:GLU_HIST + t, :])


def _conf_sample(proj, hist, prev, layer, w_dw, b_dw, ln_g, ln_b, *, b, t, dd, dc, bb):
    depth = hist.shape[0]
    col0 = (4 * dd) // dc
    in_specs = [pl.BlockSpec((bb * t, dc), lambda i: (i, col0)),
                pl.BlockSpec((bb * t, dc), lambda i: (i, col0 + 1)),
                pl.BlockSpec((bb * t, dc), lambda i: (i, col0 + 2)),
                pl.BlockSpec((1, bb, CONF_KERNEL - 1, dc), lambda i: (layer, i, 0, 0)),
                _layer_spec(w_dw, layer), _layer_spec(b_dw, layer), _layer_spec(ln_g, layer),
                _layer_spec(ln_b, layer)]
    args = [proj, proj, proj, hist, w_dw, b_dw, ln_g, ln_b]
    g_spec, slab, alias_at = _layered_out(depth, layer, prev, (bb, CONF_KERNEL - 1, dc), len(args))
    aliases = {}
    if prev is not None:
        in_specs.append(pl.BlockSpec(memory_space=pl.ANY))
        args.append(prev)
        aliases = {alias_at: 1}
    kern = functools.partial(_conf_sample_kernel, t=t, bb=bb, slab=slab)
    return pl.pallas_call(
        kern,
        out_shape=(jax.ShapeDtypeStruct((b * t, dc), BF16),
                   jax.ShapeDtypeStruct((depth, b, CONF_KERNEL - 1, dc), F32)),
        grid=(b // bb,),
        in_specs=in_specs,
        out_specs=(pl.BlockSpec((bb * t, dc), lambda i: (i, 0)), g_spec),
        scratch_shapes=[pltpu.VMEM((bb, GLU_HIST + t + SUBLANE, dc), F32),
                        pltpu.VMEM((SUBLANE, dc // LANE, bb, GLU_HIST + t, LANE), F32),
                        pltpu.VMEM((bb * t, dc), F32)],
        input_output_aliases=aliases,
        compiler_params=_params(1),
        name="conf_sample",
    )(*args)


def _outproj_kernel(x_ref, gate_ref, o_ref, u_ref, wo_ref, wu_ref, fg_ref, y_ref, *, final):
    mix = _dot(o_ref[...], wo_ref[0]) + _dot(u_ref[...], wu_ref[0])
    x = x_ref[...]
    y = x + gate_ref[...] * mix.reshape(x.shape)
    if final:
        y = y * lax.rsqrt(jnp.mean(y * y, axis=-1, keepdims=True) + EPS) * fg_ref[...]
    y_ref[...] = y


def _outproj(x, mod, o, u, w_out, final_g, layer, *, bb, tt, final):
    b, t, d = x.shape
    dd, dc = o.shape[1], u.shape[1]
    nt = t // tt
    tm = bb * tt
    kern = functools.partial(_outproj_kernel, final=final)
    return pl.pallas_call(
        kern,
        out_shape=jax.ShapeDtypeStruct((b, t, d), F32),
        grid=(b // bb, nt),
        in_specs=[pl.BlockSpec((bb, tt, d), lambda i, j: (i, j, 0)),
                  pl.BlockSpec((bb, 1, d), lambda i, j: (i, 0, 2)),
                  pl.BlockSpec((tm, dd), lambda i, j: (i * nt + j, 0)),
                  pl.BlockSpec((tm, dc), lambda i, j: (i * nt + j, 0)),
                  pl.BlockSpec((1, dd, d), lambda i, j: (layer, 0, 0)),
                  pl.BlockSpec((1, dc, d), lambda i, j: (layer, dd // dc, 0)),
                  pl.BlockSpec((1, d), lambda i, j: (0, 0))],
        out_specs=pl.BlockSpec((bb, tt, d), lambda i, j: (i, j, 0)),
        compiler_params=_params(2),
        name="out_proj",
    )(x, mod, o, u, w_out, w_out, final_g.reshape(1, d))


def kernel(x_prompt, x_sample, c_prompt, c_sample, state_delta, state_qkv_conv, state_glu_conv,
           norm_g, w_ada, b_ada, w_in, w_qkv_conv, a_log, dt_bias, head_norm_g, w_dw, b_dw,
           ln_g, ln_b, w_out, final_g):
    depth = w_in.shape[0]
    bp, tp, d = x_prompt.shape
    bs, ts, _ = x_sample.shape
    heads = a_log.shape[1]
    dd = w_qkv_conv.shape[2] // 3
    dk = dd // heads
    dc = w_dw.shape[2]
    assert ts % SUBLANE == 0 and ts >= QKV_CONV - 1 and ts <= CONF_KERNEL - 1
    assert 2 * heads <= LANE and dd % dc == 0 and tp % CHUNK == 0
    assert CHUNK % ts == 0 and bs % (CHUNK // ts) == 0 and (CHUNK // ts) % 2 == 0

    mod = _ada(jnp.concatenate([c_prompt, c_sample], axis=0), w_ada, b_ada)

    tn = min(1024, dd, dc)
    tm = 1024
    tt_in = min(tm, tp)
    bb_s = min(bs, tm // ts)
    tt_out = min(512, tp)
    bb_out = min(bs, 512 // ts)
    tt_conf = min(256, tp)
    bb_conf = min(bs, 8)
    nb_delta = 4 if bp % 4 == 0 else 1
    nseq = CHUNK // ts

    w_t = jnp.swapaxes(w_in, 1, 2)
    w_a = w_t[:, :4 * dd].astype(BF16)
    w_b = w_t[:, 4 * dd + 2 * heads:].astype(BF16)
    w_ba = jnp.pad(w_t[:, 4 * dd:4 * dd + 2 * heads], ((0, 0), (0, LANE - 2 * heads), (0, 0))).astype(BF16)
    w_out_b = w_out.astype(BF16)
    norm_g3 = norm_g.reshape(depth, 1, d)
    alog = jnp.pad(a_log, ((0, 0), (heads, LANE - 2 * heads))).reshape(depth, 1, LANE)
    dtb = jnp.pad(dt_bias, ((0, 0), (heads, LANE - 2 * heads))).reshape(depth, 1, LANE)
    hng = head_norm_g.reshape(depth, 1, dk)
    bdw, lng, lnb = b_dw.reshape(depth, 1, dc), ln_g.reshape(depth, 1, dc), ln_b.reshape(depth, 1, dc)

    hp, hs = x_prompt, x_sample
    sp_l, qp_l, gp_l = [], [], []
    s_s = q_s = g_s = None
    for l in range(depth):
        mod_p = mod[l, :bp].reshape(bp, 1, 3 * d)
        mod_s = mod[l, bp:].reshape(bs, 1, 3 * d)
        final = l == depth - 1

        proj, ba = _inproj(hp, mod_p, norm_g3, w_a, w_b, w_ba, l, bb=1, tt=tt_in, tn=tn)
        o, s_new, qbuf = _delta_prompt(proj, ba, w_qkv_conv, alog, dtb, hng, l,
                                       b=bp, t=tp, nb=nb_delta, heads=heads, dk=dk)
        u, gbuf = _conf_prompt(proj, w_dw, bdw, lng, lnb, l, b=bp, t=tp, dd=dd, dc=dc, tt=tt_conf)
        hp = _outproj(hp, mod_p, o, u, w_out_b, final_g, l, bb=1, tt=tt_out, final=final)
        sp_l.append(s_new); qp_l.append(qbuf); gp_l.append(gbuf)

        proj, ba = _inproj(hs, mod_s, norm_g3, w_a, w_b, w_ba, l, bb=bb_s, tt=ts, tn=tn)
        o, s_s, q_s = _delta_sample(proj, ba, state_qkv_conv, state_delta,
                                    None if l == 0 else (s_s, q_s), l, w_qkv_conv, alog, dtb, hng,
                                    b=bs, t=ts, nseq=nseq, heads=heads, dk=dk)
        u, g_s = _conf_sample(proj, state_glu_conv, g_s, l, w_dw, bdw, lng, lnb,
                              b=bs, t=ts, dd=dd, dc=dc, bb=bb_conf)
        hs = _outproj(hs, mod_s, o, u, w_out_b, final_g, l, bb=bb_out, tt=ts, final=final)

    return (hp, hs, jnp.stack(sp_l), jnp.stack(qp_l), jnp.stack(gp_l), s_s, q_s, g_s)
```

```python
import functools
import math

import jax
import jax.numpy as jnp
from jax import lax
from jax.experimental import pallas as pl
from jax.experimental.pallas import tpu as pltpu

F32 = jnp.float32
BF16 = jnp.bfloat16

EPS = 1e-6
QKV_CONV = 4
CONF_KERNEL = 31
CHUNK = 64
LANE = 128
SUBLANE = 8
QKV_HIST = SUBLANE
GLU_HIST = 32
CONF_PARTIALS = 3
VMEM_LIMIT_BYTES = 56 * 1024 * 1024


def _params(n_axes):
    return pltpu.CompilerParams(dimension_semantics=("arbitrary",) * n_axes,
                                vmem_limit_bytes=VMEM_LIMIT_BYTES)


def _sigmoid(x):
    return 0.5 * jnp.tanh(0.5 * x) + 0.5


def _silu(x):
    half = 0.5 * x
    return half * jnp.tanh(half) + half


def _softplus(x):
    return jnp.maximum(x, 0.0) + jnp.log1p(jnp.exp(-jnp.abs(x)))


def _dot(a, b):
    return jnp.dot(a, b, preferred_element_type=F32)


def _dot_nt(a, b):
    return lax.dot_general(a, b, (((1,), (1,)), ((), ())), preferred_element_type=F32)


def _dot_tn(a, b):
    return lax.dot_general(a, b, (((0,), (0,)), ((), ())), preferred_element_type=F32)


def _ada_kernel(c_ref, w_ref, b_ref, o_ref):
    s = _silu(c_ref[...]).astype(BF16)
    o_ref[0] = _dot(s, w_ref[0].astype(BF16)) + b_ref[0]


def _ada(c_all, w_ada, b_ada):
    depth, d, n = w_ada.shape
    rows = c_all.shape[0]
    tn = min(n, 512)
    return pl.pallas_call(
        _ada_kernel,
        out_shape=jax.ShapeDtypeStruct((depth, rows, n), F32),
        grid=(depth, n // tn),
        in_specs=[pl.BlockSpec((rows, d), lambda l, j: (0, 0)),
                  pl.BlockSpec((1, d, tn), lambda l, j: (l, 0, j)),
                  pl.BlockSpec((1, 1, tn), lambda l, j: (l, 0, j))],
        out_specs=pl.BlockSpec((1, rows, tn), lambda l, j: (l, 0, j)),
        compiler_params=_params(2),
        name="ada_mod",
    )(c_all, w_ada, b_ada.reshape(depth, 1, n))


def _inproj_kernel(x_ref, shift_ref, scale_ref, g_ref, w_ref, wba_ref, proj_ref, ba_ref, h_ref):
    @pl.when(pl.program_id(2) == 0)
    def _():
        x = x_ref[...]
        ms = jnp.mean(x * x, axis=-1, keepdims=True)
        h = x * lax.rsqrt(ms + EPS) * g_ref[0] * (1.0 + scale_ref[...]) + shift_ref[...]
        h = h.reshape(h_ref.shape).astype(BF16)
        h_ref[...] = h
        ba_ref[...] = _dot_nt(h, wba_ref[0])

    proj_ref[...] = _dot_nt(h_ref[...], w_ref[0])


def _inproj(x, mod, norm_g, w_t, w_ba, layer, *, bb, tt, tn, n_a, skip):
    b, t, d = x.shape
    n = w_t.shape[1] - skip
    nb, nt, nn = b // bb, t // tt, n // tn
    tm = bb * tt
    return pl.pallas_call(
        _inproj_kernel,
        out_shape=(jax.ShapeDtypeStruct((b * t, n), F32),
                   jax.ShapeDtypeStruct((b * t, LANE), F32)),
        grid=(nb, nt, nn),
        in_specs=[pl.BlockSpec((bb, tt, d), lambda i, j, k: (i, j, 0)),
                  pl.BlockSpec((bb, 1, d), lambda i, j, k: (i, 0, 0)),
                  pl.BlockSpec((bb, 1, d), lambda i, j, k: (i, 0, 1)),
                  pl.BlockSpec((1, 1, d), lambda i, j, k: (layer, 0, 0)),
                  pl.BlockSpec((pl.Element(1), pl.Element(tn), pl.Element(d)),
                               lambda i, j, k: (layer, pl.multiple_of(
                                   k * tn + jnp.where(k >= n_a, skip, 0), math.gcd(tn, skip)), 0)),
                  pl.BlockSpec((1, LANE, d), lambda i, j, k: (layer, 0, 0))],
        out_specs=(pl.BlockSpec((tm, tn), lambda i, j, k: (i * nt + j, k)),
                   pl.BlockSpec((tm, LANE), lambda i, j, k: (i * nt + j, 0))),
        scratch_shapes=[pltpu.VMEM((tm, d), BF16)],
        compiler_params=_params(3),
        name="in_proj",
    )(x, mod, mod, norm_g, w_t, w_ba)


def _short_conv(tiles, pick_cur, pick_prev, wconv_ref):
    cur, prev = pick_cur(tiles), pick_prev(tiles)
    sub = lax.broadcasted_iota(jnp.int32, cur.shape, cur.ndim - 2)
    acc = wconv_ref[0, QKV_CONV - 1:QKV_CONV, :] * cur
    for d in range(1, QKV_CONV):
        delayed = pltpu.roll(jnp.where(sub < SUBLANE - d, cur, prev), d, axis=1)
        acc = acc + wconv_ref[0, QKV_CONV - 1 - d:QKV_CONV - d, :] * delayed
    return acc


def _gates(ba, alog, dtb):
    beta = _sigmoid(ba)
    g = -jnp.exp(alog) * _softplus(ba + dtb)
    return beta, g


def _delta_rows(qkv, z_ref, ba, alog, dtb, hng, s_prev, *, c, nseq, heads, dk):
    rows_n = nseq * c
    dd = heads * dk
    hs = range(heads)
    rows = lax.broadcasted_iota(jnp.int32, (rows_n, rows_n), 0)
    cols = lax.broadcasted_iota(jnp.int32, (rows_n, rows_n), 1)
    if nseq > 1:
        shift = c.bit_length() - 1
        same = (rows >> shift) == (cols >> shift)
        causal = same & (rows >= cols)
        strict = same & (rows > cols)
        upper = same & (rows <= cols)
        ones = jnp.where(same, 1.0, 0.0).astype(BF16)
    else:
        causal = rows >= cols
        strict = rows > cols
        upper = rows <= cols
        ones = jnp.ones((rows_n, rows_n), BF16)
    lower_b = jnp.where(causal, 1.0, 0.0).astype(BF16)
    upper_b = jnp.where(upper, 1.0, 0.0).astype(BF16)
    eye = jnp.where(rows == cols, 1.0, 0.0)

    beta, g = _gates(ba, alog, dtb)
    g0 = g.astype(BF16)
    r1 = g - g0.astype(F32)
    g1 = r1.astype(BF16)
    g2 = (r1 - g1.astype(F32)).astype(BF16)
    gc = _dot(lower_b, g0) + (_dot(lower_b, g1) + _dot(lower_b, g2))
    gct = _dot_tn(g0, upper_b) + (_dot_tn(g1, upper_b) + _dot_tn(g2, upper_b))
    gl = _dot(ones, g0) + (_dot(ones, g1) + _dot(ones, g2))
    yield

    def l2n(x, scale):
        return x * (lax.rsqrt(jnp.sum(x * x, axis=-1, keepdims=True) + EPS) * scale)

    qn = [l2n(qkv[:, h * dk:(h + 1) * dk], dk ** -0.5) for h in hs]
    kn = [l2n(qkv[:, dd + h * dk:dd + (h + 1) * dk], 1.0) for h in hs]
    vv = [qkv[:, 2 * dd + h * dk:2 * dd + (h + 1) * dk] for h in hs]
    bcol = [beta[:, h:h + 1] for h in hs]
    gcol = [gc[:, heads + h:heads + h + 1] for h in hs]
    glc = [gl[:, heads + h:heads + h + 1] for h in hs]
    decay = [jnp.where(causal, jnp.exp(jnp.where(causal, gcol[h] - gct[heads + h:heads + h + 1, :], 0.0)), 0.0)
             for h in hs]
    kb = [kn[h].astype(BF16) for h in hs]
    kq = [jnp.concatenate([kn[h], qn[h]], axis=0).astype(BF16) for h in hs]
    kkqk = [_dot_nt(kq[h], kb[h]) for h in hs]
    yield
    low = [jnp.where(strict, bcol[h] * kkqk[h][:rows_n] * decay[h], 0.0) for h in hs]
    attn = [(kkqk[h][rows_n:] * decay[h]).astype(BF16) for h in hs]
    inv = [eye - jnp.where((rows ^ cols) == 1, low[h], 0.0) for h in hs]
    b = 2
    while b < c:
        pair = ((rows ^ cols) >= b) & ((rows ^ cols) < 2 * b)
        off = [jnp.where(pair, low[h], 0.0).astype(BF16) for h in hs]
        invb = [inv[h].astype(BF16) for h in hs]
        y = [_dot(off[h], invb[h]).astype(BF16) for h in hs]
        yield
        inv = [inv[h] - _dot(invb[h], y[h]) for h in hs]
        yield
        b *= 2
    egc = [jnp.exp(gcol[h]) for h in hs]
    rhs = [jnp.concatenate([bcol[h] * vv[h], (bcol[h] * egc[h]) * kn[h]], axis=1).astype(BF16) for h in hs]
    sol = [_dot(inv[h].astype(BF16), rhs[h]) for h in hs]
    yield
    qg = [qn[h] * egc[h] for h in hs]
    kdb = [(kn[h] * jnp.exp(glc[h] - gcol[h])).astype(BF16) for h in hs]

    if nseq == 1:
        sb = [s_prev[0][h].astype(BF16) for h in hs]
        wq = [jnp.concatenate([sol[h][:, dk:], qg[h]], axis=0).astype(BF16) for h in hs]
        ys = [_dot(wq[h], sb[h]) for h in hs]
        yield
        ub = [(sol[h][:, :dk] - ys[h][:rows_n]).astype(BF16) for h in hs]
        o = [ys[h][rows_n:] + _dot(attn[h], ub[h]) for h in hs]
        yield
        s_new = [[jnp.exp(glc[h][0:1, :]) * s_prev[0][h] + _dot_tn(kdb[h], ub[h]) for h in hs]]
    else:
        js = range(nseq)
        sb = [[s_prev[j][h].astype(BF16) for h in hs] for j in js]
        wq = [[jnp.concatenate([sol[h][j * c:(j + 1) * c, dk:], qg[h][j * c:(j + 1) * c]], axis=0).astype(BF16)
               for h in hs] for j in js]
        ys = [[_dot(wq[j][h], sb[j][h]) for h in hs] for j in js]
        ws = [jnp.concatenate([ys[j][h][:c] for j in js], axis=0) for h in hs]
        qs = [jnp.concatenate([ys[j][h][c:] for j in js], axis=0) for h in hs]
        u = [sol[h][:, :dk] - ws[h] for h in hs]
        o = [qs[h] + _dot(attn[h], u[h].astype(BF16)) for h in hs]
        seq_of_row = lax.broadcasted_iota(jnp.int32, (rows_n, dk), 0) >> (c.bit_length() - 1)
        s_new = [[None] * heads for _ in js]
        for j in range(0, nseq, 2):
            um = [jnp.concatenate([jnp.where(seq_of_row == j, u[h], 0.0),
                                   jnp.where(seq_of_row == j + 1, u[h], 0.0)], axis=1).astype(BF16) for h in hs]
            sn = [_dot_tn(kdb[h], um[h]) for h in hs]
            for h in hs:
                s_new[j][h] = jnp.exp(glc[h][j * c:j * c + 1, :]) * s_prev[j][h] + sn[h][:, :dk]
                s_new[j + 1][h] = (jnp.exp(glc[h][(j + 1) * c:(j + 1) * c + 1, :]) * s_prev[j + 1][h]
                                   + sn[h][:, dk:])
    out = []
    for h in hs:
        on = o[h] * lax.rsqrt(jnp.mean(o[h] * o[h], axis=-1, keepdims=True) + EPS) * hng
        out.append(on * _silu(z_ref[:, h * dk:(h + 1) * dk]))
    return out, s_new


def _lockstep(gens):
    results = [None] * len(gens)
    live = list(range(len(gens)))
    while live:
        for idx in list(live):
            try:
                next(gens[idx])
            except StopIteration as stop:
                results[idx] = stop.value
                live.remove(idx)
    return results


def _delta_prompt_kernel(qkv_ref, z_ref, ba_ref, wconv_ref, alog_ref, dtb_ref, hng_ref,
                         o_ref, s_ref, qbuf_ref, ext_ref, *, c, nb, heads, dk):
    i = pl.program_id(1)

    @pl.when(i == 0)
    def _():
        ext_ref[...] = jnp.zeros(ext_ref.shape, F32)
        s_ref[...] = jnp.zeros(s_ref.shape, F32)

    gens = []
    for g in range(nb):
        xe = jnp.concatenate([ext_ref[g], qkv_ref[g]], axis=0)
        tiles = xe.reshape((QKV_HIST + c) // SUBLANE, SUBLANE, xe.shape[1])
        acc = _short_conv(tiles, lambda a: a[1:], lambda a: a[:-1], wconv_ref).reshape(c, xe.shape[1])
        ext_ref[g] = qkv_ref[g, c - QKV_HIST:c, :]
        s_prev = [[s_ref[g, h] for h in range(heads)]]
        gens.append(_delta_rows(_silu(acc), z_ref.at[g], ba_ref[g], alog_ref[0], dtb_ref[0], hng_ref[0],
                                s_prev, c=c, nseq=1, heads=heads, dk=dk))
    for g, (out, s_new) in enumerate(_lockstep(gens)):
        for h in range(heads):
            o_ref[g, :, h * dk:(h + 1) * dk] = out[h].astype(o_ref.dtype)
            s_ref[g, h] = s_new[0][h]

    @pl.when(i == pl.num_programs(1) - 1)
    def _():
        qbuf_ref[...] = qkv_ref[:, c - (QKV_CONV - 1):c, :]


def _layer_spec(arr, layer):
    zeros = (0,) * (arr.ndim - 1)
    return pl.BlockSpec((1,) + arr.shape[1:], lambda *_: (layer,) + zeros)


def _delta_prompt(proj, ba, wconv, alog, dtb, hng, layer, *, b, t, nb, heads, dk):
    dd = heads * dk
    c = min(CHUNK, t)
    nc = t // c
    proj3 = proj.reshape(b, t, proj.shape[1])
    kern = functools.partial(_delta_prompt_kernel, c=c, nb=nb, heads=heads, dk=dk)
    o, s_new, qbuf = pl.pallas_call(
        kern,
        out_shape=(jax.ShapeDtypeStruct((b, t, dd), BF16),
                   jax.ShapeDtypeStruct((b, heads, dk, dk), F32),
                   jax.ShapeDtypeStruct((b, QKV_CONV - 1, 3 * dd), F32)),
        grid=(b // nb, nc),
        in_specs=[pl.BlockSpec((nb, c, 3 * dd), lambda i, j: (i, j, 0)),
                  pl.BlockSpec((nb, c, dd), lambda i, j: (i, j, 3)),
                  pl.BlockSpec((nb, c, LANE), lambda i, j: (i, j, 0)),
                  _layer_spec(wconv, layer), _layer_spec(alog, layer), _layer_spec(dtb, layer),
                  _layer_spec(hng, layer)],
        out_specs=(pl.BlockSpec((nb, c, dd), lambda i, j: (i, j, 0)),
                   pl.BlockSpec((nb, heads, dk, dk), lambda i, j: (i, 0, 0, 0)),
                   pl.BlockSpec((nb, QKV_CONV - 1, 3 * dd), lambda i, j: (i, 0, 0))),
        scratch_shapes=[pltpu.VMEM((nb, QKV_HIST, 3 * dd), F32)],
        compiler_params=_params(2),
        name="delta_prompt",
    )(proj3, proj3, ba.reshape(b, t, LANE), wconv, alog, dtb, hng)
    return o.reshape(b * t, dd), s_new, qbuf


def _write_slab(ref, slab, value):
    for other in range(ref.shape[0]):
        if other != slab:
            ref[other] = jnp.zeros(ref.shape[1:], ref.dtype)
    ref[slab] = value


def _delta_sample_kernel(qkv_ref, z_ref, ba_ref, hist_ref, s0_ref, wconv_ref, alog_ref, dtb_ref,
                         hng_ref, *rest, t, nseq, heads, dk, slab):
    o_ref, s_ref, qbuf_ref, ext_ref = rest[-4:]
    n = ext_ref.shape[2]
    first = QKV_HIST - (QKV_CONV - 1)
    ext_ref[:, 0:first, :] = jnp.zeros((nseq, first, n), F32)
    ext_ref[:, first:QKV_HIST, :] = hist_ref[0]
    ext_ref[:, QKV_HIST:QKV_HIST + t, :] = qkv_ref[...].reshape(nseq, t, n)
    per_seq = (QKV_HIST + t) // SUBLANE
    tiles = ext_ref[...].reshape(nseq * per_seq, SUBLANE, n)

    def pick(a, lo, hi):
        return a.reshape(nseq, per_seq, SUBLANE, n)[:, lo:hi].reshape(nseq * (per_seq - 1), SUBLANE, n)

    acc = _short_conv(tiles, lambda a: pick(a, 1, per_seq), lambda a: pick(a, 0, per_seq - 1),
                      wconv_ref).reshape(nseq * t, n)
    _write_slab(qbuf_ref, slab, ext_ref[:, QKV_HIST + t - (QKV_CONV - 1):QKV_HIST + t, :])

    s_prev = [[s0_ref[0, j, h] for h in range(heads)] for j in range(nseq)]
    (out, s_new), = _lockstep([_delta_rows(_silu(acc), z_ref, ba_ref[...], alog_ref[0], dtb_ref[0],
                                           hng_ref[0], s_prev, c=t, nseq=nseq, heads=heads, dk=dk)])
    for other in range(s_ref.shape[0]):
        if other != slab:
            s_ref[other] = jnp.zeros(s_ref.shape[1:], F32)
    for h in range(heads):
        o_ref[:, h * dk:(h + 1) * dk] = out[h].astype(o_ref.dtype)
        for j in range(nseq):
            s_ref[slab, j, h] = s_new[j][h]


def _layered_out(depth, layer, prev, block_tail, n_in):
    zeros = (0,) * (len(block_tail) - 1)
    if prev is None:
        return pl.BlockSpec((depth,) + block_tail, lambda i: (0, i) + zeros), layer, None
    return pl.BlockSpec((1,) + block_tail, lambda i: (layer, i) + zeros), 0, n_in


def _delta_sample(proj, ba, hist, state, prev, layer, wconv, alog, dtb, hng, *, b, t, nseq, heads, dk):
    dd = heads * dk
    depth = state.shape[0]
    rows_n = nseq * t
    in_specs = [pl.BlockSpec((rows_n, 3 * dd), lambda i: (i, 0)),
                pl.BlockSpec((rows_n, dd), lambda i: (i, 3)),
                pl.BlockSpec((rows_n, LANE), lambda i: (i, 0)),
                pl.BlockSpec((1, nseq, QKV_CONV - 1, 3 * dd), lambda i: (layer, i, 0, 0)),
                pl.BlockSpec((1, nseq, heads, dk, dk), lambda i: (layer, i, 0, 0, 0)),
                _layer_spec(wconv, layer), _layer_spec(alog, layer), _layer_spec(dtb, layer),
                _layer_spec(hng, layer)]
    args = [proj, proj, ba, hist, state, wconv, alog, dtb, hng]
    s_spec, slab, alias_at = _layered_out(depth, layer, prev, (nseq, heads, dk, dk), len(args))
    q_spec, _, _ = _layered_out(depth, layer, prev, (nseq, QKV_CONV - 1, 3 * dd), len(args))
    aliases = {}
    if prev is not None:
        in_specs += [pl.BlockSpec(memory_space=pl.ANY)] * 2
        args += list(prev)
        aliases = {alias_at: 1, alias_at + 1: 2}
    kern = functools.partial(_delta_sample_kernel, t=t, nseq=nseq, heads=heads, dk=dk, slab=slab)
    return pl.pallas_call(
        kern,
        out_shape=(jax.ShapeDtypeStruct((b * t, dd), BF16),
                   jax.ShapeDtypeStruct((depth, b, heads, dk, dk), F32),
                   jax.ShapeDtypeStruct((depth, b, QKV_CONV - 1, 3 * dd), F32)),
        grid=(b // nseq,),
        in_specs=in_specs,
        out_specs=(pl.BlockSpec((rows_n, dd), lambda i: (i, 0)), s_spec, q_spec),
        scratch_shapes=[pltpu.VMEM((nseq, QKV_HIST + t, 3 * dd), F32)],
        input_output_aliases=aliases,
        compiler_params=_params(1),
        name="delta_sample",
    )(*args)


def _conf_epilogue(conv, zc, bdw, lng, lnb):
    y = conv + bdw
    mu = jnp.mean(y, axis=-1, keepdims=True)
    yc = y - mu
    var = jnp.mean(yc * yc, axis=-1, keepdims=True)
    y = yc * lax.rsqrt(var + EPS) * lng + lnb
    return _silu(y) * _silu(zc)


def _conf_tile(ga_ref, gb_ref, zc_ref, w_ref, bdw_ref, lng_ref, lnb_ref, ext_ref, sh_ref, conv_ref,
               *, tt, rr):
    dc = ext_ref.shape[1]
    first = GLU_HIST - (CONF_KERNEL - 1)
    ext_ref[GLU_HIST:GLU_HIST + tt, :] = ga_ref[...] * _sigmoid(gb_ref[...])
    for p in range(SUBLANE):
        for lt in range(dc // LANE):
            sh_ref[p, lt] = ext_ref[p:p + GLU_HIST + tt, lt * LANE:(lt + 1) * LANE]
    for lt in range(dc // LANE):
        lanes = slice(lt * LANE, (lt + 1) * LANE)
        taps = [jnp.broadcast_to(w_ref[0, j:j + 1, lanes], (rr, LANE)) for j in range(CONF_KERNEL)]

        def body(r, carry, lt=lt, lanes=lanes, taps=taps):
            r0 = pl.multiple_of(r * rr, rr)
            parts = [None] * CONF_PARTIALS
            for j in range(CONF_KERNEL):
                off = first + j
                win = sh_ref[off % SUBLANE, lt, pl.ds(r0 + (off // SUBLANE) * SUBLANE, rr), :]
                p = j % CONF_PARTIALS
                parts[p] = taps[j] * win if parts[p] is None else parts[p] + taps[j] * win
            acc = parts[0]
            for p in range(1, CONF_PARTIALS):
                acc = acc + parts[p]
            conv_ref[pl.ds(r0, rr), lanes] = acc
            return carry

        lax.fori_loop(0, tt // rr, body, 0)
    y = _conf_epilogue(conv_ref[...], zc_ref[...], bdw_ref[0], lng_ref[0], lnb_ref[0])
    ext_ref[0:GLU_HIST, :] = ext_ref[tt:tt + GLU_HIST, :]
    return y


def _conf_prompt_kernel(ga_ref, gb_ref, zc_ref, w_ref, bdw_ref, lng_ref, lnb_ref,
                        u_ref, gbuf_ref, ext_ref, sh_ref, conv_ref, *, tt, rr):
    i = pl.program_id(1)
    dc = ext_ref.shape[1]

    @pl.when(i == 0)
    def _():
        ext_ref[0:GLU_HIST, :] = jnp.zeros((GLU_HIST, dc), F32)
        ext_ref[GLU_HIST + tt:GLU_HIST + tt + SUBLANE, :] = jnp.zeros((SUBLANE, dc), F32)

    y = _conf_tile(ga_ref, gb_ref, zc_ref, w_ref, bdw_ref, lng_ref, lnb_ref, ext_ref, sh_ref, conv_ref,
                   tt=tt, rr=rr)
    u_ref[...] = y.astype(u_ref.dtype)

    @pl.when(i == pl.num_programs(1) - 1)
    def _():
        gbuf_ref[0] = ext_ref[GLU_HIST - (CONF_KERNEL - 1):GLU_HIST, :]


def _conf_prompt(proj, w_dw, b_dw, ln_g, ln_b, layer, *, b, t, dd, dc, tt):
    nt = t // tt
    rr = min(32, tt)
    col0 = (4 * dd) // dc
    kern = functools.partial(_conf_prompt_kernel, tt=tt, rr=rr)
    row = lambda i, j: i * nt + j
    return pl.pallas_call(
        kern,
        out_shape=(jax.ShapeDtypeStruct((b * t, dc), BF16),
                   jax.ShapeDtypeStruct((b, CONF_KERNEL - 1, dc), F32)),
        grid=(b, nt),
        in_specs=[pl.BlockSpec((tt, dc), lambda i, j: (row(i, j), col0)),
                  pl.BlockSpec((tt, dc), lambda i, j: (row(i, j), col0 + 1)),
                  pl.BlockSpec((tt, dc), lambda i, j: (row(i, j), col0 + 2)),
                  _layer_spec(w_dw, layer), _layer_spec(b_dw, layer), _layer_spec(ln_g, layer),
                  _layer_spec(ln_b, layer)],
        out_specs=(pl.BlockSpec((tt, dc), lambda i, j: (row(i, j), 0)),
                   pl.BlockSpec((1, CONF_KERNEL - 1, dc), lambda i, j: (i, 0, 0))),
        scratch_shapes=[pltpu.VMEM((GLU_HIST + tt + SUBLANE, dc), F32),
                        pltpu.VMEM((SUBLANE, dc // LANE, GLU_HIST + tt, LANE), F32),
                        pltpu.VMEM((tt, dc), F32)],
        compiler_params=_params(2),
        name="conf_prompt",
    )(proj, proj, proj, w_dw, b_dw, ln_g, ln_b)


def _conf_sample_kernel(ga_ref, gb_ref, zc_ref, hist_ref, w_ref, bdw_ref, lng_ref, lnb_ref,
                        *rest, t, bb, slab):
    u_ref, gbuf_ref, ext_ref, sh_ref, conv_ref = rest[-5:]
    dc = ext_ref.shape[2]
    first = GLU_HIST - (CONF_KERNEL - 1)
    rows = GLU_HIST + t
    ext_ref[:, 0:first, :] = jnp.zeros((bb, first, dc), F32)
    ext_ref[:, first:GLU_HIST, :] = hist_ref[0]
    ext_ref[:, GLU_HIST:rows, :] = (ga_ref[...] * _sigmoid(gb_ref[...])).reshape(bb, t, dc)
    ext_ref[:, rows:rows + SUBLANE, :] = jnp.zeros((bb, SUBLANE, dc), F32)
    for p in range(SUBLANE):
        for lt in range(dc // LANE):
            sh_ref[p, lt] = ext_ref[:, p:p + rows, lt * LANE:(lt + 1) * LANE]
    for lt in range(dc // LANE):
        lanes = slice(lt * LANE, (lt + 1) * LANE)
        acc = None
        for j in range(CONF_KERNEL):
            off = first + j
            base = (off // SUBLANE) * SUBLANE
            term = w_ref[0, j:j + 1, lanes] * sh_ref[off % SUBLANE, lt, :, base:base + t, :]
            acc = term if acc is None else acc + term
        conv_ref[:, lanes] = acc.reshape(bb * t, LANE)
    y = _conf_epilogue(conv_ref[...], zc_ref[...], bdw_ref[0], lng_ref[0], lnb_ref[0])
    u_ref[...] = y.astype(u_ref.dtype)
    _write_slab(gbuf_ref, slab, ext_ref[:, first + t:GLU_HIST + t, :])


def _conf_sample(proj, hist, prev, layer, w_dw, b_dw, ln_g, ln_b, *, b, t, dd, dc, bb):
    depth = hist.shape[0]
    col0 = (4 * dd) // dc
    in_specs = [pl.BlockSpec((bb * t, dc), lambda i: (i, col0)),
                pl.BlockSpec((bb * t, dc), lambda i: (i, col0 + 1)),
                pl.BlockSpec((bb * t, dc), lambda i: (i, col0 + 2)),
                pl.BlockSpec((1, bb, CONF_KERNEL - 1, dc), lambda i: (layer, i, 0, 0)),
                _layer_spec(w_dw, layer), _layer_spec(b_dw, layer), _layer_spec(ln_g, layer),
                _layer_spec(ln_b, layer)]
    args = [proj, proj, proj, hist, w_dw, b_dw, ln_g, ln_b]
    g_spec, slab, alias_at = _layered_out(depth, layer, prev, (bb, CONF_KERNEL - 1, dc), len(args))
    aliases = {}
    if prev is not None:
        in_specs.append(pl.BlockSpec(memory_space=pl.ANY))
        args.append(prev)
        aliases = {alias_at: 1}
    kern = functools.partial(_conf_sample_kernel, t=t, bb=bb, slab=slab)
    return pl.pallas_call(
        kern,
        out_shape=(jax.ShapeDtypeStruct((b * t, dc), BF16),
                   jax.ShapeDtypeStruct((depth, b, CONF_KERNEL - 1, dc), F32)),
        grid=(b // bb,),
        in_specs=in_specs,
        out_specs=(pl.BlockSpec((bb * t, dc), lambda i: (i, 0)), g_spec),
        scratch_shapes=[pltpu.VMEM((bb, GLU_HIST + t + SUBLANE, dc), F32),
                        pltpu.VMEM((SUBLANE, dc // LANE, bb, GLU_HIST + t, LANE), F32),
                        pltpu.VMEM((bb * t, dc), F32)],
        input_output_aliases=aliases,
        compiler_params=_params(1),
        name="conf_sample",
    )(*args)


def _outproj_kernel(x_ref, gate_ref, o_ref, u_ref, wo_ref, wu_ref, fg_ref, y_ref, *, final):
    mix = _dot(o_ref[...], wo_ref[0]) + _dot(u_ref[...], wu_ref[0])
    x = x_ref[...]
    y = x + gate_ref[...] * mix.reshape(x.shape)
    if final:
        y = y * lax.rsqrt(jnp.mean(y * y, axis=-1, keepdims=True) + EPS) * fg_ref[...]
    y_ref[...] = y


def _outproj(x, mod, o, u, w_out, final_g, layer, *, bb, tt, final):
    b, t, d = x.shape
    dd, dc = o.shape[1], u.shape[1]
    nt = t // tt
    tm = bb * tt
    kern = functools.partial(_outproj_kernel, final=final)
    return pl.pallas_call(
        kern,
        out_shape=jax.ShapeDtypeStruct((b, t, d), F32),
        grid=(b // bb, nt),
        in_specs=[pl.BlockSpec((bb, tt, d), lambda i, j: (i, j, 0)),
                  pl.BlockSpec((bb, 1, d), lambda i, j: (i, 0, 2)),
                  pl.BlockSpec((tm, dd), lambda i, j: (i * nt + j, 0)),
                  pl.BlockSpec((tm, dc), lambda i, j: (i * nt + j, 0)),
                  pl.BlockSpec((1, dd, d), lambda i, j: (layer, 0, 0)),
                  pl.BlockSpec((1, dc, d), lambda i, j: (layer, dd // dc, 0)),
                  pl.BlockSpec((1, d), lambda i, j: (0, 0))],
        out_specs=pl.BlockSpec((bb, tt, d), lambda i, j: (i, j, 0)),
        compiler_params=_params(2),
        name="out_proj",
    )(x, mod, o, u, w_out, w_out, final_g.reshape(1, d))


def kernel(x_prompt, x_sample, c_prompt, c_sample, state_delta, state_qkv_conv, state_glu_conv,
           norm_g, w_ada, b_ada, w_in, w_qkv_conv, a_log, dt_bias, head_norm_g, w_dw, b_dw,
           ln_g, ln_b, w_out, final_g):
    depth = w_in.shape[0]
    bp, tp, d = x_prompt.shape
    bs, ts, _ = x_sample.shape
    heads = a_log.shape[1]
    dd = w_qkv_conv.shape[2] // 3
    dk = dd // heads
    dc = w_dw.shape[2]
    assert ts % SUBLANE == 0 and ts >= QKV_CONV - 1 and ts <= CONF_KERNEL - 1
    assert 2 * heads <= LANE and dd % dc == 0 and tp % CHUNK == 0
    assert CHUNK % ts == 0 and bs % (CHUNK // ts) == 0 and (CHUNK // ts) % 2 == 0

    mod = _ada(jnp.concatenate([c_prompt, c_sample], axis=0), w_ada, b_ada)

    tn = min(1024, dd, dc)
    tm = 1024
    tt_in = min(tm, tp)
    bb_s = min(bs, tm // ts)
    tt_out = min(512, tp)
    bb_out = min(bs, 512 // ts)
    tt_conf = min(256, tp)
    bb_conf = min(bs, 16)
    nb_delta = 4 if bp % 4 == 0 else 1
    nseq = CHUNK // ts

    w_t = jnp.swapaxes(w_in, 1, 2).astype(BF16)
    w_ba = jnp.pad(w_t[:, 4 * dd:4 * dd + 2 * heads], ((0, 0), (0, LANE - 2 * heads), (0, 0)))
    w_out_b = w_out.astype(BF16)
    norm_g3 = norm_g.reshape(depth, 1, d)
    alog = jnp.pad(a_log, ((0, 0), (heads, LANE - 2 * heads))).reshape(depth, 1, LANE)
    dtb = jnp.pad(dt_bias, ((0, 0), (heads, LANE - 2 * heads))).reshape(depth, 1, LANE)
    hng = head_norm_g.reshape(depth, 1, dk)
    bdw, lng, lnb = b_dw.reshape(depth, 1, dc), ln_g.reshape(depth, 1, dc), ln_b.reshape(depth, 1, dc)

    hp, hs = x_prompt, x_sample
    sp_l, qp_l, gp_l = [], [], []
    s_s = q_s = g_s = None
    for l in range(depth):
        mod_p = mod[l, :bp].reshape(bp, 1, 3 * d)
        mod_s = mod[l, bp:].reshape(bs, 1, 3 * d)
        final = l == depth - 1

        proj, ba = _inproj(hp, mod_p, norm_g3, w_t, w_ba, l, bb=1, tt=tt_in, tn=tn,
                           n_a=4 * dd // tn, skip=2 * heads)
        o, s_new, qbuf = _delta_prompt(proj, ba, w_qkv_conv, alog, dtb, hng, l,
                                       b=bp, t=tp, nb=nb_delta, heads=heads, dk=dk)
        u, gbuf = _conf_prompt(proj, w_dw, bdw, lng, lnb, l, b=bp, t=tp, dd=dd, dc=dc, tt=tt_conf)
        hp = _outproj(hp, mod_p, o, u, w_out_b, final_g, l, bb=1, tt=tt_out, final=final)
        sp_l.append(s_new); qp_l.append(qbuf); gp_l.append(gbuf)

        proj, ba = _inproj(hs, mod_s, norm_g3, w_t, w_ba, l, bb=bb_s, tt=ts, tn=tn,
                           n_a=4 * dd // tn, skip=2 * heads)
        o, s_s, q_s = _delta_sample(proj, ba, state_qkv_conv, state_delta,
                                    None if l == 0 else (s_s, q_s), l, w_qkv_conv, alog, dtb, hng,
                                    b=bs, t=ts, nseq=nseq, heads=heads, dk=dk)
        u, g_s = _conf_sample(proj, state_glu_conv, g_s, l, w_dw, bdw, lng, lnb,
                              b=bs, t=ts, dd=dd, dc=dc, bb=bb_conf)
        hs = _outproj(hs, mod_s, o, u, w_out_b, final_g, l, bb=bb_out, tt=ts, final=final)

    return (hp, hs, jnp.stack(sp_l), jnp.stack(qp_l), jnp.stack(gp_l), s_s, q_s, g_s)
```

```python
import functools
import math

import jax
import jax.numpy as jnp
from jax import lax
from jax.experimental import pallas as pl
from jax.experimental.pallas import tpu as pltpu

F32 = jnp.float32
BF16 = jnp.bfloat16

EPS = 1e-6
QKV_CONV = 4
CONF_KERNEL = 31
CHUNK = 64
LANE = 128
SUBLANE = 8
QKV_HIST = SUBLANE
GLU_HIST = 32
CONF_PARTIALS = 3
VMEM_LIMIT_BYTES = 56 * 1024 * 1024


def _params(n_axes):
    return pltpu.CompilerParams(dimension_semantics=("arbitrary",) * n_axes,
                                vmem_limit_bytes=VMEM_LIMIT_BYTES)


def _sigmoid(x):
    return 0.5 * jnp.tanh(0.5 * x) + 0.5


def _silu(x):
    half = 0.5 * x
    return half * jnp.tanh(half) + half


def _softplus(x):
    return jnp.maximum(x, 0.0) + jnp.log1p(jnp.exp(-jnp.abs(x)))


def _dot(a, b):
    return jnp.dot(a, b, preferred_element_type=F32)


def _dot_nt(a, b):
    return lax.dot_general(a, b, (((1,), (1,)), ((), ())), preferred_element_type=F32)


def _dot_tn(a, b):
    return lax.dot_general(a, b, (((0,), (0,)), ((), ())), preferred_element_type=F32)


def _ada_kernel(c_ref, w_ref, b_ref, o_ref):
    s = _silu(c_ref[...]).astype(BF16)
    o_ref[0] = _dot(s, w_ref[0].astype(BF16)) + b_ref[0]


def _ada(c_all, w_ada, b_ada):
    depth, d, n = w_ada.shape
    rows = c_all.shape[0]
    tn = min(n, 512)
    return pl.pallas_call(
        _ada_kernel,
        out_shape=jax.ShapeDtypeStruct((depth, rows, n), F32),
        grid=(depth, n // tn),
        in_specs=[pl.BlockSpec((rows, d), lambda l, j: (0, 0)),
                  pl.BlockSpec((1, d, tn), lambda l, j: (l, 0, j)),
                  pl.BlockSpec((1, 1, tn), lambda l, j: (l, 0, j))],
        out_specs=pl.BlockSpec((1, rows, tn), lambda l, j: (l, 0, j)),
        compiler_params=_params(2),
        name="ada_mod",
    )(c_all, w_ada, b_ada.reshape(depth, 1, n))


def _inproj_kernel(x_ref, shift_ref, scale_ref, g_ref, w_ref, wba_ref, proj_ref, ba_ref, h_ref):
    @pl.when(pl.program_id(2) == 0)
    def _():
        x = x_ref[...]
        ms = jnp.mean(x * x, axis=-1, keepdims=True)
        h = x * lax.rsqrt(ms + EPS) * g_ref[0] * (1.0 + scale_ref[...]) + shift_ref[...]
        h = h.reshape(h_ref.shape).astype(BF16)
        h_ref[...] = h
        ba_ref[...] = _dot_nt(h, wba_ref[0].astype(BF16))

    proj_ref[...] = _dot_nt(h_ref[...], w_ref[0].astype(BF16))


def _inproj(x, mod, norm_g, w_t, w_ba, layer, *, bb, tt, tn, n_a, skip):
    b, t, d = x.shape
    n = w_t.shape[1] - skip
    nb, nt, nn = b // bb, t // tt, n // tn
    tm = bb * tt
    return pl.pallas_call(
        _inproj_kernel,
        out_shape=(jax.ShapeDtypeStruct((b * t, n), F32),
                   jax.ShapeDtypeStruct((b * t, LANE), F32)),
        grid=(nb, nt, nn),
        in_specs=[pl.BlockSpec((bb, tt, d), lambda i, j, k: (i, j, 0)),
                  pl.BlockSpec((bb, 1, d), lambda i, j, k: (i, 0, 0)),
                  pl.BlockSpec((bb, 1, d), lambda i, j, k: (i, 0, 1)),
                  pl.BlockSpec((1, 1, d), lambda i, j, k: (layer, 0, 0)),
                  pl.BlockSpec((pl.Element(1), pl.Element(tn), pl.Element(d)),
                               lambda i, j, k: (layer, pl.multiple_of(
                                   k * tn + jnp.where(k >= n_a, skip, 0), math.gcd(tn, skip)), 0)),
                  pl.BlockSpec((1, LANE, d), lambda i, j, k: (layer, 0, 0))],
        out_specs=(pl.BlockSpec((tm, tn), lambda i, j, k: (i * nt + j, k)),
                   pl.BlockSpec((tm, LANE), lambda i, j, k: (i * nt + j, 0))),
        scratch_shapes=[pltpu.VMEM((tm, d), BF16)],
        compiler_params=_params(3),
        name="in_proj",
    )(x, mod, mod, norm_g, w_t, w_ba)


def _short_conv(tiles, pick_cur, pick_prev, wconv_ref):
    cur, prev = pick_cur(tiles), pick_prev(tiles)
    sub = lax.broadcasted_iota(jnp.int32, cur.shape, cur.ndim - 2)
    acc = wconv_ref[0, QKV_CONV - 1:QKV_CONV, :] * cur
    for d in range(1, QKV_CONV):
        delayed = pltpu.roll(jnp.where(sub < SUBLANE - d, cur, prev), d, axis=1)
        acc = acc + wconv_ref[0, QKV_CONV - 1 - d:QKV_CONV - d, :] * delayed
    return acc


def _gates(ba, alog, dtb):
    beta = _sigmoid(ba)
    g = -jnp.exp(alog) * _softplus(ba + dtb)
    return beta, g


def _delta_rows(qkv, z_ref, ba, alog, dtb, hng, s_prev, *, c, nseq, heads, dk):
    rows_n = nseq * c
    dd = heads * dk
    hs = range(heads)
    rows = lax.broadcasted_iota(jnp.int32, (rows_n, rows_n), 0)
    cols = lax.broadcasted_iota(jnp.int32, (rows_n, rows_n), 1)
    if nseq > 1:
        shift = c.bit_length() - 1
        same = (rows >> shift) == (cols >> shift)
        causal = same & (rows >= cols)
        strict = same & (rows > cols)
        upper = same & (rows <= cols)
        ones = jnp.where(same, 1.0, 0.0).astype(BF16)
    else:
        causal = rows >= cols
        strict = rows > cols
        upper = rows <= cols
        ones = jnp.ones((rows_n, rows_n), BF16)
    lower_b = jnp.where(causal, 1.0, 0.0).astype(BF16)
    upper_b = jnp.where(upper, 1.0, 0.0).astype(BF16)
    eye = jnp.where(rows == cols, 1.0, 0.0)

    beta, g = _gates(ba, alog, dtb)
    g0 = g.astype(BF16)
    r1 = g - g0.astype(F32)
    g1 = r1.astype(BF16)
    g2 = (r1 - g1.astype(F32)).astype(BF16)
    gc = _dot(lower_b, g0) + (_dot(lower_b, g1) + _dot(lower_b, g2))
    gct = _dot_tn(g0, upper_b) + (_dot_tn(g1, upper_b) + _dot_tn(g2, upper_b))
    gl = _dot(ones, g0) + (_dot(ones, g1) + _dot(ones, g2))
    yield

    def l2n(x, scale):
        return x * (lax.rsqrt(jnp.sum(x * x, axis=-1, keepdims=True) + EPS) * scale)

    qn = [l2n(qkv[:, h * dk:(h + 1) * dk], dk ** -0.5) for h in hs]
    kn = [l2n(qkv[:, dd + h * dk:dd + (h + 1) * dk], 1.0) for h in hs]
    vv = [qkv[:, 2 * dd + h * dk:2 * dd + (h + 1) * dk] for h in hs]
    bcol = [beta[:, h:h + 1] for h in hs]
    gcol = [gc[:, heads + h:heads + h + 1] for h in hs]
    glc = [gl[:, heads + h:heads + h + 1] for h in hs]
    decay = [jnp.where(causal, jnp.exp(jnp.where(causal, gcol[h] - gct[heads + h:heads + h + 1, :], 0.0)), 0.0)
             for h in hs]
    kb = [kn[h].astype(BF16) for h in hs]
    kq = [jnp.concatenate([kn[h], qn[h]], axis=0).astype(BF16) for h in hs]
    kkqk = [_dot_nt(kq[h], kb[h]) for h in hs]
    yield
    low = [jnp.where(strict, bcol[h] * kkqk[h][:rows_n] * decay[h], 0.0) for h in hs]
    attn = [(kkqk[h][rows_n:] * decay[h]).astype(BF16) for h in hs]
    inv = [eye - jnp.where((rows ^ cols) == 1, low[h], 0.0) for h in hs]
    b = 2
    while b < c:
        pair = ((rows ^ cols) >= b) & ((rows ^ cols) < 2 * b)
        off = [jnp.where(pair, low[h], 0.0).astype(BF16) for h in hs]
        invb = [inv[h].astype(BF16) for h in hs]
        y = [_dot(off[h], invb[h]).astype(BF16) for h in hs]
        yield
        inv = [inv[h] - _dot(invb[h], y[h]) for h in hs]
        yield
        b *= 2
    egc = [jnp.exp(gcol[h]) for h in hs]
    rhs = [jnp.concatenate([bcol[h] * vv[h], (bcol[h] * egc[h]) * kn[h]], axis=1).astype(BF16) for h in hs]
    sol = [_dot(inv[h].astype(BF16), rhs[h]) for h in hs]
    yield
    qg = [qn[h] * egc[h] for h in hs]
    kdb = [(kn[h] * jnp.exp(glc[h] - gcol[h])).astype(BF16) for h in hs]

    if nseq == 1:
        sb = [s_prev[0][h].astype(BF16) for h in hs]
        wq = [jnp.concatenate([sol[h][:, dk:], qg[h]], axis=0).astype(BF16) for h in hs]
        ys = [_dot(wq[h], sb[h]) for h in hs]
        yield
        ub = [(sol[h][:, :dk] - ys[h][:rows_n]).astype(BF16) for h in hs]
        o = [ys[h][rows_n:] + _dot(attn[h], ub[h]) for h in hs]
        yield
        s_new = [[jnp.exp(glc[h][0:1, :]) * s_prev[0][h] + _dot_tn(kdb[h], ub[h]) for h in hs]]
    else:
        js = range(nseq)
        sb = [[s_prev[j][h].astype(BF16) for h in hs] for j in js]
        wq = [[jnp.concatenate([sol[h][j * c:(j + 1) * c, dk:], qg[h][j * c:(j + 1) * c]], axis=0).astype(BF16)
               for h in hs] for j in js]
        ys = [[_dot(wq[j][h], sb[j][h]) for h in hs] for j in js]
        ws = [jnp.concatenate([ys[j][h][:c] for j in js], axis=0) for h in hs]
        qs = [jnp.concatenate([ys[j][h][c:] for j in js], axis=0) for h in hs]
        u = [sol[h][:, :dk] - ws[h] for h in hs]
        o = [qs[h] + _dot(attn[h], u[h].astype(BF16)) for h in hs]
        seq_of_row = lax.broadcasted_iota(jnp.int32, (rows_n, dk), 0) >> (c.bit_length() - 1)
        s_new = [[None] * heads for _ in js]
        for j in range(0, nseq, 2):
            um = [jnp.concatenate([jnp.where(seq_of_row == j, u[h], 0.0),
                                   jnp.where(seq_of_row == j + 1, u[h], 0.0)], axis=1).astype(BF16) for h in hs]
            sn = [_dot_tn(kdb[h], um[h]) for h in hs]
            for h in hs:
                s_new[j][h] = jnp.exp(glc[h][j * c:j * c + 1, :]) * s_prev[j][h] + sn[h][:, :dk]
                s_new[j + 1][h] = (jnp.exp(glc[h][(j + 1) * c:(j + 1) * c + 1, :]) * s_prev[j + 1][h]
                                   + sn[h][:, dk:])
    out = []
    for h in hs:
        on = o[h] * lax.rsqrt(jnp.mean(o[h] * o[h], axis=-1, keepdims=True) + EPS) * hng
        out.append(on * _silu(z_ref[:, h * dk:(h + 1) * dk]))
    return out, s_new


def _lockstep(gens):
    results = [None] * len(gens)
    live = list(range(len(gens)))
    while live:
        for idx in list(live):
            try:
                next(gens[idx])
            except StopIteration as stop:
                results[idx] = stop.value
                live.remove(idx)
    return results


def _delta_prompt_kernel(qkv_ref, z_ref, ba_ref, wconv_ref, alog_ref, dtb_ref, hng_ref,
                         o_ref, s_ref, qbuf_ref, ext_ref, *, c, nb, heads, dk):
    i = pl.program_id(1)

    @pl.when(i == 0)
    def _():
        ext_ref[...] = jnp.zeros(ext_ref.shape, F32)
        s_ref[...] = jnp.zeros(s_ref.shape, F32)

    gens = []
    for g in range(nb):
        xe = jnp.concatenate([ext_ref[g], qkv_ref[g]], axis=0)
        tiles = xe.reshape((QKV_HIST + c) // SUBLANE, SUBLANE, xe.shape[1])
        acc = _short_conv(tiles, lambda a: a[1:], lambda a: a[:-1], wconv_ref).reshape(c, xe.shape[1])
        ext_ref[g] = qkv_ref[g, c - QKV_HIST:c, :]
        s_prev = [[s_ref[g, h] for h in range(heads)]]
        gens.append(_delta_rows(_silu(acc), z_ref.at[g], ba_ref[g], alog_ref[0], dtb_ref[0], hng_ref[0],
                                s_prev, c=c, nseq=1, heads=heads, dk=dk))
    for g, (out, s_new) in enumerate(_lockstep(gens)):
        for h in range(heads):
            o_ref[g, :, h * dk:(h + 1) * dk] = out[h].astype(o_ref.dtype)
            s_ref[g, h] = s_new[0][h]

    @pl.when(i == pl.num_programs(1) - 1)
    def _():
        qbuf_ref[...] = qkv_ref[:, c - (QKV_CONV - 1):c, :]


def _layer_spec(arr, layer):
    zeros = (0,) * (arr.ndim - 1)
    return pl.BlockSpec((1,) + arr.shape[1:], lambda *_: (layer,) + zeros)


def _delta_prompt(proj, ba, wconv, alog, dtb, hng, layer, *, b, t, nb, heads, dk):
    dd = heads * dk
    c = min(CHUNK, t)
    nc = t // c
    proj3 = proj.reshape(b, t, proj.shape[1])
    kern = functools.partial(_delta_prompt_kernel, c=c, nb=nb, heads=heads, dk=dk)
    o, s_new, qbuf = pl.pallas_call(
        kern,
        out_shape=(jax.ShapeDtypeStruct((b, t, dd), BF16),
                   jax.ShapeDtypeStruct((b, heads, dk, dk), F32),
                   jax.ShapeDtypeStruct((b, QKV_CONV - 1, 3 * dd), F32)),
        grid=(b // nb, nc),
        in_specs=[pl.BlockSpec((nb, c, 3 * dd), lambda i, j: (i, j, 0)),
                  pl.BlockSpec((nb, c, dd), lambda i, j: (i, j, 3)),
                  pl.BlockSpec((nb, c, LANE), lambda i, j: (i, j, 0)),
                  _layer_spec(wconv, layer), _layer_spec(alog, layer), _layer_spec(dtb, layer),
                  _layer_spec(hng, layer)],
        out_specs=(pl.BlockSpec((nb, c, dd), lambda i, j: (i, j, 0)),
                   pl.BlockSpec((nb, heads, dk, dk), lambda i, j: (i, 0, 0, 0)),
                   pl.BlockSpec((nb, QKV_CONV - 1, 3 * dd), lambda i, j: (i, 0, 0))),
        scratch_shapes=[pltpu.VMEM((nb, QKV_HIST, 3 * dd), F32)],
        compiler_params=_params(2),
        name="delta_prompt",
    )(proj3, proj3, ba.reshape(b, t, LANE), wconv, alog, dtb, hng)
    return o.reshape(b * t, dd), s_new, qbuf


def _write_slab(ref, slab, value):
    for other in range(ref.shape[0]):
        if other != slab:
            ref[other] = jnp.zeros(ref.shape[1:], ref.dtype)
    ref[slab] = value


def _delta_sample_kernel(qkv_ref, z_ref, ba_ref, hist_ref, s0_ref, wconv_ref, alog_ref, dtb_ref,
                         hng_ref, *rest, t, nseq, heads, dk, slab):
    o_ref, s_ref, qbuf_ref, ext_ref = rest[-4:]
    n = ext_ref.shape[2]
    first = QKV_HIST - (QKV_CONV - 1)
    ext_ref[:, 0:first, :] = jnp.zeros((nseq, first, n), F32)
    for r in range(QKV_CONV - 1):
        ext_ref[:, first + r, :] = hist_ref[0, r]
    ext_ref[:, QKV_HIST:QKV_HIST + t, :] = qkv_ref[...].reshape(nseq, t, n)
    per_seq = (QKV_HIST + t) // SUBLANE
    tiles = ext_ref[...].reshape(nseq * per_seq, SUBLANE, n)

    def pick(a, lo, hi):
        return a.reshape(nseq, per_seq, SUBLANE, n)[:, lo:hi].reshape(nseq * (per_seq - 1), SUBLANE, n)

    acc = _short_conv(tiles, lambda a: pick(a, 1, per_seq), lambda a: pick(a, 0, per_seq - 1),
                      wconv_ref).reshape(nseq * t, n)
    _write_slab(qbuf_ref, slab, jnp.stack(
        [ext_ref[:, QKV_HIST + t - (QKV_CONV - 1) + r, :] for r in range(QKV_CONV - 1)]))

    s_prev = [[s0_ref[0, j, h] for h in range(heads)] for j in range(nseq)]
    (out, s_new), = _lockstep([_delta_rows(_silu(acc), z_ref, ba_ref[...], alog_ref[0], dtb_ref[0],
                                           hng_ref[0], s_prev, c=t, nseq=nseq, heads=heads, dk=dk)])
    for other in range(s_ref.shape[0]):
        if other != slab:
            s_ref[other] = jnp.zeros(s_ref.shape[1:], F32)
    for h in range(heads):
        o_ref[:, h * dk:(h + 1) * dk] = out[h].astype(o_ref.dtype)
        for j in range(nseq):
            s_ref[slab, j, h] = s_new[j][h]


def _layered_out(depth, layer, prev, block_tail, n_in, batch_pos=0):
    def index(first, i):
        tail = [0] * len(block_tail)
        tail[batch_pos] = i
        return (first,) + tuple(tail)

    if prev is None:
        return pl.BlockSpec((depth,) + block_tail, lambda i: index(0, i)), layer, None
    return pl.BlockSpec((1,) + block_tail, lambda i: index(layer, i)), 0, n_in


def _delta_sample(proj, ba, hist, state, prev, layer, wconv, alog, dtb, hng, *, b, t, nseq, heads, dk):
    dd = heads * dk
    depth = state.shape[0]
    rows_n = nseq * t
    in_specs = [pl.BlockSpec((rows_n, 3 * dd), lambda i: (i, 0)),
                pl.BlockSpec((rows_n, dd), lambda i: (i, 3)),
                pl.BlockSpec((rows_n, LANE), lambda i: (i, 0)),
                pl.BlockSpec((1, QKV_CONV - 1, nseq, 3 * dd), lambda i: (layer, 0, i, 0)),
                pl.BlockSpec((1, nseq, heads, dk, dk), lambda i: (layer, i, 0, 0, 0)),
                _layer_spec(wconv, layer), _layer_spec(alog, layer), _layer_spec(dtb, layer),
                _layer_spec(hng, layer)]
    args = [proj, proj, ba, hist, state, wconv, alog, dtb, hng]
    s_spec, slab, alias_at = _layered_out(depth, layer, prev, (nseq, heads, dk, dk), len(args))
    q_spec, _, _ = _layered_out(depth, layer, prev, (QKV_CONV - 1, nseq, 3 * dd), len(args), batch_pos=1)
    aliases = {}
    if prev is not None:
        in_specs += [pl.BlockSpec(memory_space=pl.ANY)] * 2
        args += list(prev)
        aliases = {alias_at: 1, alias_at + 1: 2}
    kern = functools.partial(_delta_sample_kernel, t=t, nseq=nseq, heads=heads, dk=dk, slab=slab)
    return pl.pallas_call(
        kern,
        out_shape=(jax.ShapeDtypeStruct((b * t, dd), BF16),
                   jax.ShapeDtypeStruct((depth, b, heads, dk, dk), F32),
                   jax.ShapeDtypeStruct((depth, QKV_CONV - 1, b, 3 * dd), F32)),
        grid=(b // nseq,),
        in_specs=in_specs,
        out_specs=(pl.BlockSpec((rows_n, dd), lambda i: (i, 0)), s_spec, q_spec),
        scratch_shapes=[pltpu.VMEM((nseq, QKV_HIST + t, 3 * dd), F32)],
        input_output_aliases=aliases,
        compiler_params=_params(1),
        name="delta_sample",
    )(*args)


def _conf_epilogue(conv, zc, bdw, lng, lnb):
    y = conv + bdw
    mu = jnp.mean(y, axis=-1, keepdims=True)
    yc = y - mu
    var = jnp.mean(yc * yc, axis=-1, keepdims=True)
    y = yc * lax.rsqrt(var + EPS) * lng + lnb
    return _silu(y) * _silu(zc)


def _conf_tile(ga_ref, gb_ref, zc_ref, w_ref, bdw_ref, lng_ref, lnb_ref, ext_ref, sh_ref, conv_ref,
               *, tt, rr):
    dc = ext_ref.shape[1]
    first = GLU_HIST - (CONF_KERNEL - 1)
    ext_ref[GLU_HIST:GLU_HIST + tt, :] = ga_ref[...] * _sigmoid(gb_ref[...])
    for p in range(SUBLANE):
        for lt in range(dc // LANE):
            sh_ref[p, lt] = ext_ref[p:p + GLU_HIST + tt, lt * LANE:(lt + 1) * LANE]
    for lt in range(dc // LANE):
        lanes = slice(lt * LANE, (lt + 1) * LANE)
        taps = [jnp.broadcast_to(w_ref[0, j:j + 1, lanes], (rr, LANE)) for j in range(CONF_KERNEL)]

        def body(r, carry, lt=lt, lanes=lanes, taps=taps):
            r0 = pl.multiple_of(r * rr, rr)
            parts = [None] * CONF_PARTIALS
            for j in range(CONF_KERNEL):
                off = first + j
                win = sh_ref[off % SUBLANE, lt, pl.ds(r0 + (off // SUBLANE) * SUBLANE, rr), :]
                p = j % CONF_PARTIALS
                parts[p] = taps[j] * win if parts[p] is None else parts[p] + taps[j] * win
            acc = parts[0]
            for p in range(1, CONF_PARTIALS):
                acc = acc + parts[p]
            conv_ref[pl.ds(r0, rr), lanes] = acc
            return carry

        lax.fori_loop(0, tt // rr, body, 0)
    y = _conf_epilogue(conv_ref[...], zc_ref[...], bdw_ref[0], lng_ref[0], lnb_ref[0])
    ext_ref[0:GLU_HIST, :] = ext_ref[tt:tt + GLU_HIST, :]
    return y


def _conf_prompt_kernel(ga_ref, gb_ref, zc_ref, w_ref, bdw_ref, lng_ref, lnb_ref,
                        u_ref, gbuf_ref, ext_ref, sh_ref, conv_ref, *, tt, rr):
    i = pl.program_id(1)
    dc = ext_ref.shape[1]

    @pl.when(i == 0)
    def _():
        ext_ref[0:GLU_HIST, :] = jnp.zeros((GLU_HIST, dc), F32)
        ext_ref[GLU_HIST + tt:GLU_HIST + tt + SUBLANE, :] = jnp.zeros((SUBLANE, dc), F32)

    y = _conf_tile(ga_ref, gb_ref, zc_ref, w_ref, bdw_ref, lng_ref, lnb_ref, ext_ref, sh_ref, conv_ref,
                   tt=tt, rr=rr)
    u_ref[...] = y.astype(u_ref.dtype)

    @pl.when(i == pl.num_programs(1) - 1)
    def _():
        gbuf_ref[0] = ext_ref[GLU_HIST - (CONF_KERNEL - 1):GLU_HIST, :]


def _conf_prompt(proj, w_dw, b_dw, ln_g, ln_b, layer, *, b, t, dd, dc, tt):
    nt = t // tt
    rr = min(32, tt)
    col0 = (4 * dd) // dc
    kern = functools.partial(_conf_prompt_kernel, tt=tt, rr=rr)
    row = lambda i, j: i * nt + j
    return pl.pallas_call(
        kern,
        out_shape=(jax.ShapeDtypeStruct((b * t, dc), BF16),
                   jax.ShapeDtypeStruct((b, CONF_KERNEL - 1, dc), F32)),
        grid=(b, nt),
        in_specs=[pl.BlockSpec((tt, dc), lambda i, j: (row(i, j), col0)),
                  pl.BlockSpec((tt, dc), lambda i, j: (row(i, j), col0 + 1)),
                  pl.BlockSpec((tt, dc), lambda i, j: (row(i, j), col0 + 2)),
                  _layer_spec(w_dw, layer), _layer_spec(b_dw, layer), _layer_spec(ln_g, layer),
                  _layer_spec(ln_b, layer)],
        out_specs=(pl.BlockSpec((tt, dc), lambda i, j: (row(i, j), 0)),
                   pl.BlockSpec((1, CONF_KERNEL - 1, dc), lambda i, j: (i, 0, 0))),
        scratch_shapes=[pltpu.VMEM((GLU_HIST + tt + SUBLANE, dc), F32),
                        pltpu.VMEM((SUBLANE, dc // LANE, GLU_HIST + tt, LANE), F32),
                        pltpu.VMEM((tt, dc), F32)],
        compiler_params=_params(2),
        name="conf_prompt",
    )(proj, proj, proj, w_dw, b_dw, ln_g, ln_b)


def _conf_sample_kernel(ga_ref, gb_ref, zc_ref, hist_ref, w_ref, bdw_ref, lng_ref, lnb_ref,
                        *rest, t, bb, slab):
    u_ref, gbuf_ref, ext_ref, sh_ref, conv_ref = rest[-5:]
    dc = ext_ref.shape[2]
    first = GLU_HIST - (CONF_KERNEL - 1)
    rows = GLU_HIST + t
    ext_ref[:, 0:first, :] = jnp.zeros((bb, first, dc), F32)
    for r in range(CONF_KERNEL - 1):
        ext_ref[:, first + r, :] = hist_ref[0, r]
    ext_ref[:, GLU_HIST:rows, :] = (ga_ref[...] * _sigmoid(gb_ref[...])).reshape(bb, t, dc)
    ext_ref[:, rows:rows + SUBLANE, :] = jnp.zeros((bb, SUBLANE, dc), F32)
    for p in range(SUBLANE):
        for lt in range(dc // LANE):
            sh_ref[p, lt] = ext_ref[:, p:p + rows, lt * LANE:(lt + 1) * LANE]
    for lt in range(dc // LANE):
        lanes = slice(lt * LANE, (lt + 1) * LANE)
        acc = None
        for j in range(CONF_KERNEL):
            off = first + j
            base = (off // SUBLANE) * SUBLANE
            term = w_ref[0, j:j + 1, lanes] * sh_ref[off % SUBLANE, lt, :, base:base + t, :]
            acc = term if acc is None else acc + term
        conv_ref[:, lanes] = acc.reshape(bb * t, LANE)
    y = _conf_epilogue(conv_ref[...], zc_ref[...], bdw_ref[0], lng_ref[0], lnb_ref[0])
    u_ref[...] = y.astype(u_ref.dtype)
    _write_slab(gbuf_ref, slab, jnp.stack([ext_ref[:, first + t + r, :] for r in range(CONF_KERNEL - 1)]))


def _conf_sample(proj, hist, prev, layer, w_dw, b_dw, ln_g, ln_b, *, b, t, dd, dc, bb):
    depth = hist.shape[0]
    col0 = (4 * dd) // dc
    in_specs = [pl.BlockSpec((bb * t, dc), lambda i: (i, col0)),
                pl.BlockSpec((bb * t, dc), lambda i: (i, col0 + 1)),
                pl.BlockSpec((bb * t, dc), lambda i: (i, col0 + 2)),
                pl.BlockSpec((1, CONF_KERNEL - 1, bb, dc), lambda i: (layer, 0, i, 0)),
                _layer_spec(w_dw, layer), _layer_spec(b_dw, layer), _layer_spec(ln_g, layer),
                _layer_spec(ln_b, layer)]
    args = [proj, proj, proj, hist, w_dw, b_dw, ln_g, ln_b]
    g_spec, slab, alias_at = _layered_out(depth, layer, prev, (CONF_KERNEL - 1, bb, dc), len(args),
                                          batch_pos=1)
    aliases = {}
    if prev is not None:
        in_specs.append(pl.BlockSpec(memory_space=pl.ANY))
        args.append(prev)
        aliases = {alias_at: 1}
    kern = functools.partial(_conf_sample_kernel, t=t, bb=bb, slab=slab)
    return pl.pallas_call(
        kern,
        out_shape=(jax.ShapeDtypeStruct((b * t, dc), BF16),
                   jax.ShapeDtypeStruct((depth, CONF_KERNEL - 1, b, dc), F32)),
        grid=(b // bb,),
        in_specs=in_specs,
        out_specs=(pl.BlockSpec((bb * t, dc), lambda i: (i, 0)), g_spec),
        scratch_shapes=[pltpu.VMEM((bb, GLU_HIST + t + SUBLANE, dc), F32),
                        pltpu.VMEM((SUBLANE, dc // LANE, bb, GLU_HIST + t, LANE), F32),
                        pltpu.VMEM((bb * t, dc), F32)],
        input_output_aliases=aliases,
        compiler_params=_params(1),
        name="conf_sample",
    )(*args)


def _outproj_kernel(x_ref, gate_ref, o_ref, u_ref, wo_ref, wu_ref, fg_ref, y_ref, *, final):
    mix = _dot(o_ref[...], wo_ref[0]) + _dot(u_ref[...], wu_ref[0])
    x = x_ref[...]
    y = x + gate_ref[...] * mix.reshape(x.shape)
    if final:
        y = y * lax.rsqrt(jnp.mean(y * y, axis=-1, keepdims=True) + EPS) * fg_ref[...]
    y_ref[...] = y


def _outproj(x, mod, o, u, w_out, final_g, layer, *, bb, tt, final):
    b, t, d = x.shape
    dd, dc = o.shape[1], u.shape[1]
    nt = t // tt
    tm = bb * tt
    kern = functools.partial(_outproj_kernel, final=final)
    return pl.pallas_call(
        kern,
        out_shape=jax.ShapeDtypeStruct((b, t, d), F32),
        grid=(b // bb, nt),
        in_specs=[pl.BlockSpec((bb, tt, d), lambda i, j: (i, j, 0)),
                  pl.BlockSpec((bb, 1, d), lambda i, j: (i, 0, 2)),
                  pl.BlockSpec((tm, dd), lambda i, j: (i * nt + j, 0)),
                  pl.BlockSpec((tm, dc), lambda i, j: (i * nt + j, 0)),
                  pl.BlockSpec((1, dd, d), lambda i, j: (layer, 0, 0)),
                  pl.BlockSpec((1, dc, d), lambda i, j: (layer, dd // dc, 0)),
                  pl.BlockSpec((1, d), lambda i, j: (0, 0))],
        out_specs=pl.BlockSpec((bb, tt, d), lambda i, j: (i, j, 0)),
        compiler_params=_params(2),
        name="out_proj",
    )(x, mod, o, u, w_out, w_out, final_g.reshape(1, d))


def kernel(x_prompt, x_sample, c_prompt, c_sample, state_delta, state_qkv_conv, state_glu_conv,
           norm_g, w_ada, b_ada, w_in, w_qkv_conv, a_log, dt_bias, head_norm_g, w_dw, b_dw,
           ln_g, ln_b, w_out, final_g):
    depth = w_in.shape[0]
    bp, tp, d = x_prompt.shape
    bs, ts, _ = x_sample.shape
    heads = a_log.shape[1]
    dd = w_qkv_conv.shape[2] // 3
    dk = dd // heads
    dc = w_dw.shape[2]
    assert ts % SUBLANE == 0 and ts >= QKV_CONV - 1 and ts <= CONF_KERNEL - 1
    assert 2 * heads <= LANE and dd % dc == 0 and tp % CHUNK == 0
    assert CHUNK % ts == 0 and bs % (CHUNK // ts) == 0 and (CHUNK // ts) % 2 == 0

    mod = _ada(jnp.concatenate([c_prompt, c_sample], axis=0), w_ada, b_ada)

    tn = min(1024, dd, dc)
    tm = 1024
    tt_in = min(tm, tp)
    bb_s = min(bs, tm // ts)
    tt_out = min(512, tp)
    bb_out = min(bs, 512 // ts)
    tt_conf = min(256, tp)
    bb_conf = min(bs, 16)
    nb_delta = 4 if bp % 4 == 0 else 1
    nseq = CHUNK // ts

    w_t = jnp.swapaxes(w_in, 1, 2)
    w_ba = jnp.pad(w_t[:, 4 * dd:4 * dd + 2 * heads], ((0, 0), (0, LANE - 2 * heads), (0, 0)))
    w_out_b = w_out.astype(BF16)
    qkv_hist = jnp.swapaxes(state_qkv_conv, 1, 2)
    glu_hist = jnp.swapaxes(state_glu_conv, 1, 2)
    norm_g3 = norm_g.reshape(depth, 1, d)
    alog = jnp.pad(a_log, ((0, 0), (heads, LANE - 2 * heads))).reshape(depth, 1, LANE)
    dtb = jnp.pad(dt_bias, ((0, 0), (heads, LANE - 2 * heads))).reshape(depth, 1, LANE)
    hng = head_norm_g.reshape(depth, 1, dk)
    bdw, lng, lnb = b_dw.reshape(depth, 1, dc), ln_g.reshape(depth, 1, dc), ln_b.reshape(depth, 1, dc)

    hp, hs = x_prompt, x_sample
    sp_l, qp_l, gp_l = [], [], []
    s_s = q_s = g_s = None
    for l in range(depth):
        mod_p = mod[l, :bp].reshape(bp, 1, 3 * d)
        mod_s = mod[l, bp:].reshape(bs, 1, 3 * d)
        final = l == depth - 1

        proj, ba = _inproj(hp, mod_p, norm_g3, w_t, w_ba, l, bb=1, tt=tt_in, tn=tn,
                           n_a=4 * dd // tn, skip=2 * heads)
        o, s_new, qbuf = _delta_prompt(proj, ba, w_qkv_conv, alog, dtb, hng, l,
                                       b=bp, t=tp, nb=nb_delta, heads=heads, dk=dk)
        u, gbuf = _conf_prompt(proj, w_dw, bdw, lng, lnb, l, b=bp, t=tp, dd=dd, dc=dc, tt=tt_conf)
        hp = _outproj(hp, mod_p, o, u, w_out_b, final_g, l, bb=1, tt=tt_out, final=final)
        sp_l.append(s_new); qp_l.append(qbuf); gp_l.append(gbuf)

        proj, ba = _inproj(hs, mod_s, norm_g3, w_t, w_ba, l, bb=bb_s, tt=ts, tn=tn,
                           n_a=4 * dd // tn, skip=2 * heads)
        o, s_s, q_s = _delta_sample(proj, ba, qkv_hist, state_delta,
                                    None if l == 0 else (s_s, q_s), l, w_qkv_conv, alog, dtb, hng,
                                    b=bs, t=ts, nseq=nseq, heads=heads, dk=dk)
        u, g_s = _conf_sample(proj, glu_hist, g_s, l, w_dw, bdw, lng, lnb,
                              b=bs, t=ts, dd=dd, dc=dc, bb=bb_conf)
        hs = _outproj(hs, mod_s, o, u, w_out_b, final_g, l, bb=bb_out, tt=ts, final=final)

    return (hp, hs, jnp.stack(sp_l), jnp.stack(qp_l), jnp.stack(gp_l),
            s_s, jnp.swapaxes(q_s, 1, 2), jnp.swapaxes(g_s, 1, 2))
```

```python
import functools
import math

import jax
import jax.numpy as jnp
from jax import lax
from jax.experimental import pallas as pl
from jax.experimental.pallas import tpu as pltpu

F32 = jnp.float32
BF16 = jnp.bfloat16

EPS = 1e-6
QKV_CONV = 4
CONF_KERNEL = 31
CHUNK = 64
LANE = 128
SUBLANE = 8
QKV_HIST = SUBLANE
GLU_HIST = 32
CONF_PARTIALS = 3
VMEM_LIMIT_BYTES = 56 * 1024 * 1024


def _params(n_axes):
    return pltpu.CompilerParams(dimension_semantics=("arbitrary",) * n_axes,
                                vmem_limit_bytes=VMEM_LIMIT_BYTES)


def _sigmoid(x):
    return 0.5 * jnp.tanh(0.5 * x) + 0.5


def _silu(x):
    half = 0.5 * x
    return half * jnp.tanh(half) + half


def _softplus(x):
    return jnp.maximum(x, 0.0) + jnp.log1p(jnp.exp(-jnp.abs(x)))


def _dot(a, b):
    return jnp.dot(a, b, preferred_element_type=F32)


def _dot_nt(a, b):
    return lax.dot_general(a, b, (((1,), (1,)), ((), ())), preferred_element_type=F32)


def _dot_tn(a, b):
    return lax.dot_general(a, b, (((0,), (0,)), ((), ())), preferred_element_type=F32)


def _ada_kernel(c_ref, w_ref, b_ref, o_ref):
    s = _silu(c_ref[...]).astype(BF16)
    o_ref[0] = _dot(s, w_ref[0].astype(BF16)) + b_ref[0]


def _ada(c_all, w_ada, b_ada):
    depth, d, n = w_ada.shape
    rows = c_all.shape[0]
    tn = n // 4 if n % (4 * LANE) == 0 else n
    return pl.pallas_call(
        _ada_kernel,
        out_shape=jax.ShapeDtypeStruct((depth, rows, n), F32),
        grid=(depth, n // tn),
        in_specs=[pl.BlockSpec((rows, d), lambda l, j: (0, 0)),
                  pl.BlockSpec((1, d, tn), lambda l, j: (l, 0, j)),
                  pl.BlockSpec((1, 1, tn), lambda l, j: (l, 0, j))],
        out_specs=pl.BlockSpec((1, rows, tn), lambda l, j: (l, 0, j)),
        compiler_params=_params(2),
        name="ada_mod",
    )(c_all, w_ada, b_ada.reshape(depth, 1, n))


def _inproj_kernel(x_ref, shift_ref, scale_ref, g_ref, w_ref, wba_ref, proj_ref, ba_ref, h_ref):
    @pl.when(pl.program_id(2) == 0)
    def _():
        x = x_ref[...]
        ms = jnp.mean(x * x, axis=-1, keepdims=True)
        h = x * lax.rsqrt(ms + EPS) * g_ref[0] * (1.0 + scale_ref[...]) + shift_ref[...]
        h = h.reshape(h_ref.shape).astype(BF16)
        h_ref[...] = h
        ba_ref[...] = _dot_nt(h, wba_ref[0].astype(BF16))

    proj_ref[...] = _dot_nt(h_ref[...], w_ref[0].astype(BF16))


def _inproj(x, mod, norm_g, w_t, w_ba, layer, *, bb, tt, tn, n_a, skip):
    b, t, d = x.shape
    n = w_t.shape[1] - skip
    nb, nt, nn = b // bb, t // tt, n // tn
    tm = bb * tt
    return pl.pallas_call(
        _inproj_kernel,
        out_shape=(jax.ShapeDtypeStruct((b * t, n), F32),
                   jax.ShapeDtypeStruct((b * t, LANE), F32)),
        grid=(nb, nt, nn),
        in_specs=[pl.BlockSpec((bb, tt, d), lambda i, j, k: (i, j, 0)),
                  pl.BlockSpec((bb, 1, d), lambda i, j, k: (i, 0, 0)),
                  pl.BlockSpec((bb, 1, d), lambda i, j, k: (i, 0, 1)),
                  pl.BlockSpec((1, 1, d), lambda i, j, k: (layer, 0, 0)),
                  pl.BlockSpec((pl.Element(1), pl.Element(tn), pl.Element(d)),
                               lambda i, j, k: (layer, pl.multiple_of(
                                   k * tn + jnp.where(k >= n_a, skip, 0), math.gcd(tn, skip)), 0)),
                  pl.BlockSpec((1, LANE, d), lambda i, j, k: (layer, 0, 0))],
        out_specs=(pl.BlockSpec((tm, tn), lambda i, j, k: (i * nt + j, k)),
                   pl.BlockSpec((tm, LANE), lambda i, j, k: (i * nt + j, 0))),
        scratch_shapes=[pltpu.VMEM((tm, d), BF16)],
        compiler_params=_params(3),
        name="in_proj",
    )(x, mod, mod, norm_g, w_t, w_ba)


def _short_conv(tiles, pick_cur, pick_prev, wconv_ref):
    cur, prev = pick_cur(tiles), pick_prev(tiles)
    sub = lax.broadcasted_iota(jnp.int32, cur.shape, cur.ndim - 2)
    acc = wconv_ref[0, QKV_CONV - 1:QKV_CONV, :] * cur
    for d in range(1, QKV_CONV):
        delayed = pltpu.roll(jnp.where(sub < SUBLANE - d, cur, prev), d, axis=1)
        acc = acc + wconv_ref[0, QKV_CONV - 1 - d:QKV_CONV - d, :] * delayed
    return acc


def _gates(ba, alog, dtb):
    beta = _sigmoid(ba)
    g = -jnp.exp(alog) * _softplus(ba + dtb)
    return beta, g


def _delta_rows(qkv, z_ref, ba, alog, dtb, hng, s_prev, *, c, nseq, heads, dk):
    rows_n = nseq * c
    dd = heads * dk
    hs = range(heads)
    rows = lax.broadcasted_iota(jnp.int32, (rows_n, rows_n), 0)
    cols = lax.broadcasted_iota(jnp.int32, (rows_n, rows_n), 1)
    if nseq > 1:
        shift = c.bit_length() - 1
        same = (rows >> shift) == (cols >> shift)
        causal = same & (rows >= cols)
        strict = same & (rows > cols)
        upper = same & (rows <= cols)
        ones = jnp.where(same, 1.0, 0.0).astype(BF16)
    else:
        causal = rows >= cols
        strict = rows > cols
        upper = rows <= cols
        ones = jnp.ones((rows_n, rows_n), BF16)
    lower_b = jnp.where(causal, 1.0, 0.0).astype(BF16)
    upper_b = jnp.where(upper, 1.0, 0.0).astype(BF16)
    eye = jnp.where(rows == cols, 1.0, 0.0)

    beta, g = _gates(ba, alog, dtb)
    g0 = g.astype(BF16)
    r1 = g - g0.astype(F32)
    g1 = r1.astype(BF16)
    g2 = (r1 - g1.astype(F32)).astype(BF16)
    gc = _dot(lower_b, g0) + (_dot(lower_b, g1) + _dot(lower_b, g2))
    gct = _dot_tn(g0, upper_b) + (_dot_tn(g1, upper_b) + _dot_tn(g2, upper_b))
    gl = _dot(ones, g0) + (_dot(ones, g1) + _dot(ones, g2))
    yield

    def l2n(x, scale):
        return x * (lax.rsqrt(jnp.sum(x * x, axis=-1, keepdims=True) + EPS) * scale)

    qn = [l2n(qkv[:, h * dk:(h + 1) * dk], dk ** -0.5) for h in hs]
    kn = [l2n(qkv[:, dd + h * dk:dd + (h + 1) * dk], 1.0) for h in hs]
    vv = [qkv[:, 2 * dd + h * dk:2 * dd + (h + 1) * dk] for h in hs]
    bcol = [beta[:, h:h + 1] for h in hs]
    gcol = [gc[:, heads + h:heads + h + 1] for h in hs]
    glc = [gl[:, heads + h:heads + h + 1] for h in hs]
    decay = [jnp.where(causal, jnp.exp(jnp.where(causal, gcol[h] - gct[heads + h:heads + h + 1, :], 0.0)), 0.0)
             for h in hs]
    kb = [kn[h].astype(BF16) for h in hs]
    kq = [jnp.concatenate([kn[h], qn[h]], axis=0).astype(BF16) for h in hs]
    kkqk = [_dot_nt(kq[h], kb[h]) for h in hs]
    yield
    low = [jnp.where(strict, bcol[h] * kkqk[h][:rows_n] * decay[h], 0.0) for h in hs]
    attn = [(kkqk[h][rows_n:] * decay[h]).astype(BF16) for h in hs]
    inv = [eye - jnp.where((rows ^ cols) == 1, low[h], 0.0) for h in hs]
    b = 2
    while b < c:
        pair = ((rows ^ cols) >= b) & ((rows ^ cols) < 2 * b)
        off = [jnp.where(pair, low[h], 0.0).astype(BF16) for h in hs]
        invb = [inv[h].astype(BF16) for h in hs]
        y = [_dot(off[h], invb[h]).astype(BF16) for h in hs]
        yield
        inv = [inv[h] - _dot(invb[h], y[h]) for h in hs]
        yield
        b *= 2
    egc = [jnp.exp(gcol[h]) for h in hs]
    rhs = [jnp.concatenate([bcol[h] * vv[h], (bcol[h] * egc[h]) * kn[h]], axis=1).astype(BF16) for h in hs]
    sol = [_dot(inv[h].astype(BF16), rhs[h]) for h in hs]
    yield
    qg = [qn[h] * egc[h] for h in hs]
    kdb = [(kn[h] * jnp.exp(glc[h] - gcol[h])).astype(BF16) for h in hs]

    if nseq == 1:
        sb = [s_prev[0][h].astype(BF16) for h in hs]
        wq = [jnp.concatenate([sol[h][:, dk:], qg[h]], axis=0).astype(BF16) for h in hs]
        ys = [_dot(wq[h], sb[h]) for h in hs]
        yield
        ub = [(sol[h][:, :dk] - ys[h][:rows_n]).astype(BF16) for h in hs]
        o = [ys[h][rows_n:] + _dot(attn[h], ub[h]) for h in hs]
        yield
        s_new = [[jnp.exp(glc[h][0:1, :]) * s_prev[0][h] + _dot_tn(kdb[h], ub[h]) for h in hs]]
    else:
        js = range(nseq)
        sb = [[s_prev[j][h].astype(BF16) for h in hs] for j in js]
        wq = [[jnp.concatenate([sol[h][j * c:(j + 1) * c, dk:], qg[h][j * c:(j + 1) * c]], axis=0).astype(BF16)
               for h in hs] for j in js]
        ys = [[_dot(wq[j][h], sb[j][h]) for h in hs] for j in js]
        ws = [jnp.concatenate([ys[j][h][:c] for j in js], axis=0) for h in hs]
        qs = [jnp.concatenate([ys[j][h][c:] for j in js], axis=0) for h in hs]
        u = [sol[h][:, :dk] - ws[h] for h in hs]
        o = [qs[h] + _dot(attn[h], u[h].astype(BF16)) for h in hs]
        seq_of_row = lax.broadcasted_iota(jnp.int32, (rows_n, dk), 0) >> (c.bit_length() - 1)
        s_new = [[None] * heads for _ in js]
        for j in range(0, nseq, 2):
            um = [jnp.concatenate([jnp.where(seq_of_row == j, u[h], 0.0),
                                   jnp.where(seq_of_row == j + 1, u[h], 0.0)], axis=1).astype(BF16) for h in hs]
            sn = [_dot_tn(kdb[h], um[h]) for h in hs]
            for h in hs:
                s_new[j][h] = jnp.exp(glc[h][j * c:j * c + 1, :]) * s_prev[j][h] + sn[h][:, :dk]
                s_new[j + 1][h] = (jnp.exp(glc[h][(j + 1) * c:(j + 1) * c + 1, :]) * s_prev[j + 1][h]
                                   + sn[h][:, dk:])
    out = []
    for h in hs:
        on = o[h] * lax.rsqrt(jnp.mean(o[h] * o[h], axis=-1, keepdims=True) + EPS) * hng
        out.append(on * _silu(z_ref[:, h * dk:(h + 1) * dk]))
    return out, s_new


def _lockstep(gens):
    results = [None] * len(gens)
    live = list(range(len(gens)))
    while live:
        for idx in list(live):
            try:
                next(gens[idx])
            except StopIteration as stop:
                results[idx] = stop.value
                live.remove(idx)
    return results


def _delta_prompt_kernel(qkv_ref, z_ref, ba_ref, wconv_ref, alog_ref, dtb_ref, hng_ref,
                         o_ref, s_ref, qbuf_ref, ext_ref, *, c, nb, heads, dk):
    i = pl.program_id(1)

    @pl.when(i == 0)
    def _():
        ext_ref[...] = jnp.zeros(ext_ref.shape, F32)
        s_ref[...] = jnp.zeros(s_ref.shape, F32)

    gens = []
    for g in range(nb):
        xe = jnp.concatenate([ext_ref[g], qkv_ref[g]], axis=0)
        tiles = xe.reshape((QKV_HIST + c) // SUBLANE, SUBLANE, xe.shape[1])
        acc = _short_conv(tiles, lambda a: a[1:], lambda a: a[:-1], wconv_ref).reshape(c, xe.shape[1])
        ext_ref[g] = qkv_ref[g, c - QKV_HIST:c, :]
        s_prev = [[s_ref[g, h] for h in range(heads)]]
        gens.append(_delta_rows(_silu(acc), z_ref.at[g], ba_ref[g], alog_ref[0], dtb_ref[0], hng_ref[0],
                                s_prev, c=c, nseq=1, heads=heads, dk=dk))
    for g, (out, s_new) in enumerate(_lockstep(gens)):
        for h in range(heads):
            o_ref[g, :, h * dk:(h + 1) * dk] = out[h].astype(o_ref.dtype)
            s_ref[g, h] = s_new[0][h]

    @pl.when(i == pl.num_programs(1) - 1)
    def _():
        qbuf_ref[...] = qkv_ref[:, c - (QKV_CONV - 1):c, :]


def _layer_spec(arr, layer):
    zeros = (0,) * (arr.ndim - 1)
    return pl.BlockSpec((1,) + arr.shape[1:], lambda *_: (layer,) + zeros)


def _delta_prompt(proj, ba, wconv, alog, dtb, hng, layer, *, b, t, nb, heads, dk):
    dd = heads * dk
    c = min(CHUNK, t)
    nc = t // c
    proj3 = proj.reshape(b, t, proj.shape[1])
    kern = functools.partial(_delta_prompt_kernel, c=c, nb=nb, heads=heads, dk=dk)
    o, s_new, qbuf = pl.pallas_call(
        kern,
        out_shape=(jax.ShapeDtypeStruct((b, t, dd), BF16),
                   jax.ShapeDtypeStruct((b, heads, dk, dk), F32),
                   jax.ShapeDtypeStruct((b, QKV_CONV - 1, 3 * dd), F32)),
        grid=(b // nb, nc),
        in_specs=[pl.BlockSpec((nb, c, 3 * dd), lambda i, j: (i, j, 0)),
                  pl.BlockSpec((nb, c, dd), lambda i, j: (i, j, 3)),
                  pl.BlockSpec((nb, c, LANE), lambda i, j: (i, j, 0)),
                  _layer_spec(wconv, layer), _layer_spec(alog, layer), _layer_spec(dtb, layer),
                  _layer_spec(hng, layer)],
        out_specs=(pl.BlockSpec((nb, c, dd), lambda i, j: (i, j, 0)),
                   pl.BlockSpec((nb, heads, dk, dk), lambda i, j: (i, 0, 0, 0)),
                   pl.BlockSpec((nb, QKV_CONV - 1, 3 * dd), lambda i, j: (i, 0, 0))),
        scratch_shapes=[pltpu.VMEM((nb, QKV_HIST, 3 * dd), F32)],
        compiler_params=_params(2),
        name="delta_prompt",
    )(proj3, proj3, ba.reshape(b, t, LANE), wconv, alog, dtb, hng)
    return o.reshape(b * t, dd), s_new, qbuf


def _write_slab(ref, slab, value):
    for other in range(ref.shape[0]):
        if other != slab:
            ref[other] = jnp.zeros(ref.shape[1:], ref.dtype)
    ref[slab] = value


def _delta_sample_kernel(qkv_ref, z_ref, ba_ref, hist_ref, s0_ref, wconv_ref, alog_ref, dtb_ref,
                         hng_ref, *rest, t, nseq, heads, dk, slab):
    o_ref, s_ref, qbuf_ref, ext_ref = rest[-4:]
    n = ext_ref.shape[2]
    first = QKV_HIST - (QKV_CONV - 1)
    ext_ref[:, 0:first, :] = jnp.zeros((nseq, first, n), F32)
    for r in range(QKV_CONV - 1):
        ext_ref[:, first + r, :] = hist_ref[0, r]
    ext_ref[:, QKV_HIST:QKV_HIST + t, :] = qkv_ref[...].reshape(nseq, t, n)
    per_seq = (QKV_HIST + t) // SUBLANE
    tiles = ext_ref[...].reshape(nseq * per_seq, SUBLANE, n)

    def pick(a, lo, hi):
        return a.reshape(nseq, per_seq, SUBLANE, n)[:, lo:hi].reshape(nseq * (per_seq - 1), SUBLANE, n)

    acc = _short_conv(tiles, lambda a: pick(a, 1, per_seq), lambda a: pick(a, 0, per_seq - 1),
                      wconv_ref).reshape(nseq * t, n)
    _write_slab(qbuf_ref, slab, jnp.stack(
        [ext_ref[:, QKV_HIST + t - (QKV_CONV - 1) + r, :] for r in range(QKV_CONV - 1)]))

    s_prev = [[s0_ref[0, j, h] for h in range(heads)] for j in range(nseq)]
    (out, s_new), = _lockstep([_delta_rows(_silu(acc), z_ref, ba_ref[...], alog_ref[0], dtb_ref[0],
                                           hng_ref[0], s_prev, c=t, nseq=nseq, heads=heads, dk=dk)])
    for other in range(s_ref.shape[0]):
        if other != slab:
            s_ref[other] = jnp.zeros(s_ref.shape[1:], F32)
    for h in range(heads):
        o_ref[:, h * dk:(h + 1) * dk] = out[h].astype(o_ref.dtype)
        for j in range(nseq):
            s_ref[slab, j, h] = s_new[j][h]


def _layered_out(depth, layer, prev, block_tail, n_in, batch_pos=0):
    def index(first, i):
        tail = [0] * len(block_tail)
        tail[batch_pos] = i
        return (first,) + tuple(tail)

    if prev is None:
        return pl.BlockSpec((depth,) + block_tail, lambda i: index(0, i)), layer, None
    return pl.BlockSpec((1,) + block_tail, lambda i: index(layer, i)), 0, n_in


def _delta_sample(proj, ba, hist, state, prev, layer, wconv, alog, dtb, hng, *, b, t, nseq, heads, dk):
    dd = heads * dk
    depth = state.shape[0]
    rows_n = nseq * t
    in_specs = [pl.BlockSpec((rows_n, 3 * dd), lambda i: (i, 0)),
                pl.BlockSpec((rows_n, dd), lambda i: (i, 3)),
                pl.BlockSpec((rows_n, LANE), lambda i: (i, 0)),
                pl.BlockSpec((1, QKV_CONV - 1, nseq, 3 * dd), lambda i: (layer, 0, i, 0)),
                pl.BlockSpec((1, nseq, heads, dk, dk), lambda i: (layer, i, 0, 0, 0)),
                _layer_spec(wconv, layer), _layer_spec(alog, layer), _layer_spec(dtb, layer),
                _layer_spec(hng, layer)]
    args = [proj, proj, ba, hist, state, wconv, alog, dtb, hng]
    s_spec, slab, alias_at = _layered_out(depth, layer, prev, (nseq, heads, dk, dk), len(args))
    q_spec, _, _ = _layered_out(depth, layer, prev, (QKV_CONV - 1, nseq, 3 * dd), len(args), batch_pos=1)
    aliases = {}
    if prev is not None:
        in_specs += [pl.BlockSpec(memory_space=pl.ANY)] * 2
        args += list(prev)
        aliases = {alias_at: 1, alias_at + 1: 2}
    kern = functools.partial(_delta_sample_kernel, t=t, nseq=nseq, heads=heads, dk=dk, slab=slab)
    return pl.pallas_call(
        kern,
        out_shape=(jax.ShapeDtypeStruct((b * t, dd), BF16),
                   jax.ShapeDtypeStruct((depth, b, heads, dk, dk), F32),
                   jax.ShapeDtypeStruct((depth, QKV_CONV - 1, b, 3 * dd), F32)),
        grid=(b // nseq,),
        in_specs=in_specs,
        out_specs=(pl.BlockSpec((rows_n, dd), lambda i: (i, 0)), s_spec, q_spec),
        scratch_shapes=[pltpu.VMEM((nseq, QKV_HIST + t, 3 * dd), F32)],
        input_output_aliases=aliases,
        compiler_params=_params(1),
        name="delta_sample",
    )(*args)


def _conf_epilogue(conv, zc, bdw, lng, lnb):
    y = conv + bdw
    mu = jnp.mean(y, axis=-1, keepdims=True)
    yc = y - mu
    var = jnp.mean(yc * yc, axis=-1, keepdims=True)
    y = yc * lax.rsqrt(var + EPS) * lng + lnb
    return _silu(y) * _silu(zc)


def _conf_tile(ga_ref, gb_ref, zc_ref, w_ref, bdw_ref, lng_ref, lnb_ref, ext_ref, sh_ref, conv_ref,
               *, tt, rr):
    dc = ext_ref.shape[1]
    first = GLU_HIST - (CONF_KERNEL - 1)
    ext_ref[GLU_HIST:GLU_HIST + tt, :] = ga_ref[...] * _sigmoid(gb_ref[...])
    for p in range(SUBLANE):
        for lt in range(dc // LANE):
            sh_ref[p, lt] = ext_ref[p:p + GLU_HIST + tt, lt * LANE:(lt + 1) * LANE]
    for lt in range(dc // LANE):
        lanes = slice(lt * LANE, (lt + 1) * LANE)
        taps = [jnp.broadcast_to(w_ref[0, j:j + 1, lanes], (rr, LANE)) for j in range(CONF_KERNEL)]

        def body(r, carry, lt=lt, lanes=lanes, taps=taps):
            r0 = pl.multiple_of(r * rr, rr)
            parts = [None] * CONF_PARTIALS
            for j in range(CONF_KERNEL):
                off = first + j
                win = sh_ref[off % SUBLANE, lt, pl.ds(r0 + (off // SUBLANE) * SUBLANE, rr), :]
                p = j % CONF_PARTIALS
                parts[p] = taps[j] * win if parts[p] is None else parts[p] + taps[j] * win
            acc = parts[0]
            for p in range(1, CONF_PARTIALS):
                acc = acc + parts[p]
            conv_ref[pl.ds(r0, rr), lanes] = acc
            return carry

        lax.fori_loop(0, tt // rr, body, 0)
    y = _conf_epilogue(conv_ref[...], zc_ref[...], bdw_ref[0], lng_ref[0], lnb_ref[0])
    ext_ref[0:GLU_HIST, :] = ext_ref[tt:tt + GLU_HIST, :]
    return y


def _conf_prompt_kernel(ga_ref, gb_ref, zc_ref, w_ref, bdw_ref, lng_ref, lnb_ref,
                        u_ref, gbuf_ref, ext_ref, sh_ref, conv_ref, *, tt, rr):
    i = pl.program_id(1)
    dc = ext_ref.shape[1]

    @pl.when(i == 0)
    def _():
        ext_ref[0:GLU_HIST, :] = jnp.zeros((GLU_HIST, dc), F32)
        ext_ref[GLU_HIST + tt:GLU_HIST + tt + SUBLANE, :] = jnp.zeros((SUBLANE, dc), F32)

    y = _conf_tile(ga_ref, gb_ref, zc_ref, w_ref, bdw_ref, lng_ref, lnb_ref, ext_ref, sh_ref, conv_ref,
                   tt=tt, rr=rr)
    u_ref[...] = y.astype(u_ref.dtype)

    @pl.when(i == pl.num_programs(1) - 1)
    def _():
        gbuf_ref[0] = ext_ref[GLU_HIST - (CONF_KERNEL - 1):GLU_HIST, :]


def _conf_prompt(proj, w_dw, b_dw, ln_g, ln_b, layer, *, b, t, dd, dc, tt):
    nt = t // tt
    rr = min(32, tt)
    col0 = (4 * dd) // dc
    kern = functools.partial(_conf_prompt_kernel, tt=tt, rr=rr)
    row = lambda i, j: i * nt + j
    return pl.pallas_call(
        kern,
        out_shape=(jax.ShapeDtypeStruct((b * t, dc), BF16),
                   jax.ShapeDtypeStruct((b, CONF_KERNEL - 1, dc), F32)),
        grid=(b, nt),
        in_specs=[pl.BlockSpec((tt, dc), lambda i, j: (row(i, j), col0)),
                  pl.BlockSpec((tt, dc), lambda i, j: (row(i, j), col0 + 1)),
                  pl.BlockSpec((tt, dc), lambda i, j: (row(i, j), col0 + 2)),
                  _layer_spec(w_dw, layer), _layer_spec(b_dw, layer), _layer_spec(ln_g, layer),
                  _layer_spec(ln_b, layer)],
        out_specs=(pl.BlockSpec((tt, dc), lambda i, j: (row(i, j), 0)),
                   pl.BlockSpec((1, CONF_KERNEL - 1, dc), lambda i, j: (i, 0, 0))),
        scratch_shapes=[pltpu.VMEM((GLU_HIST + tt + SUBLANE, dc), F32),
                        pltpu.VMEM((SUBLANE, dc // LANE, GLU_HIST + tt, LANE), F32),
                        pltpu.VMEM((tt, dc), F32)],
        compiler_params=_params(2),
        name="conf_prompt",
    )(proj, proj, proj, w_dw, b_dw, ln_g, ln_b)


def _conf_sample_kernel(ga_ref, gb_ref, zc_ref, hist_ref, w_ref, bdw_ref, lng_ref, lnb_ref,
                        *rest, t, bb, slab):
    u_ref, gbuf_ref, ext_ref, sh_ref, conv_ref = rest[-5:]
    dc = ext_ref.shape[2]
    first = GLU_HIST - (CONF_KERNEL - 1)
    rows = GLU_HIST + t
    ext_ref[:, 0:first, :] = jnp.zeros((bb, first, dc), F32)
    for s in range(bb):
        ext_ref[s, first:GLU_HIST, :] = hist_ref[0, :, s, :]
    ext_ref[:, GLU_HIST:rows, :] = (ga_ref[...] * _sigmoid(gb_ref[...])).reshape(bb, t, dc)
    ext_ref[:, rows:rows + SUBLANE, :] = jnp.zeros((bb, SUBLANE, dc), F32)
    for p in range(SUBLANE):
        for lt in range(dc // LANE):
            sh_ref[p, lt] = ext_ref[:, p:p + rows, lt * LANE:(lt + 1) * LANE]
    for lt in range(dc // LANE):
        lanes = slice(lt * LANE, (lt + 1) * LANE)
        acc = None
        for j in range(CONF_KERNEL):
            off = first + j
            base = (off // SUBLANE) * SUBLANE
            term = w_ref[0, j:j + 1, lanes] * sh_ref[off % SUBLANE, lt, :, base:base + t, :]
            acc = term if acc is None else acc + term
        conv_ref[:, lanes] = acc.reshape(bb * t, LANE)
    y = _conf_epilogue(conv_ref[...], zc_ref[...], bdw_ref[0], lng_ref[0], lnb_ref[0])
    u_ref[...] = y.astype(u_ref.dtype)
    _write_slab(gbuf_ref, slab, jnp.stack([ext_ref[:, first + t + r, :] for r in range(CONF_KERNEL - 1)]))


def _conf_sample(proj, hist, prev, layer, w_dw, b_dw, ln_g, ln_b, *, b, t, dd, dc, bb):
    depth = hist.shape[0]
    col0 = (4 * dd) // dc
    in_specs = [pl.BlockSpec((bb * t, dc), lambda i: (i, col0)),
                pl.BlockSpec((bb * t, dc), lambda i: (i, col0 + 1)),
                pl.BlockSpec((bb * t, dc), lambda i: (i, col0 + 2)),
                pl.BlockSpec((1, CONF_KERNEL - 1, bb, dc), lambda i: (layer, 0, i, 0)),
                _layer_spec(w_dw, layer), _layer_spec(b_dw, layer), _layer_spec(ln_g, layer),
                _layer_spec(ln_b, layer)]
    args = [proj, proj, proj, hist, w_dw, b_dw, ln_g, ln_b]
    g_spec, slab, alias_at = _layered_out(depth, layer, prev, (CONF_KERNEL - 1, bb, dc), len(args),
                                          batch_pos=1)
    aliases = {}
    if prev is not None:
        in_specs.append(pl.BlockSpec(memory_space=pl.ANY))
        args.append(prev)
        aliases = {alias_at: 1}
    kern = functools.partial(_conf_sample_kernel, t=t, bb=bb, slab=slab)
    return pl.pallas_call(
        kern,
        out_shape=(jax.ShapeDtypeStruct((b * t, dc), BF16),
                   jax.ShapeDtypeStruct((depth, CONF_KERNEL - 1, b, dc), F32)),
        grid=(b // bb,),
        in_specs=in_specs,
        out_specs=(pl.BlockSpec((bb * t, dc), lambda i: (i, 0)), g_spec),
        scratch_shapes=[pltpu.VMEM((bb, GLU_HIST + t + SUBLANE, dc), F32),
                        pltpu.VMEM((SUBLANE, dc // LANE, bb, GLU_HIST + t, LANE), F32),
                        pltpu.VMEM((bb * t, dc), F32)],
        input_output_aliases=aliases,
        compiler_params=_params(1),
        name="conf_sample",
    )(*args)


def _outproj_kernel(x_ref, gate_ref, o_ref, u_ref, wo_ref, wu_ref, fg_ref, y_ref, *, final):
    mix = _dot(o_ref[...], wo_ref[0]) + _dot(u_ref[...], wu_ref[0])
    x = x_ref[...]
    y = x + gate_ref[...] * mix.reshape(x.shape)
    if final:
        y = y * lax.rsqrt(jnp.mean(y * y, axis=-1, keepdims=True) + EPS) * fg_ref[...]
    y_ref[...] = y


def _outproj(x, mod, o, u, w_out, final_g, layer, *, bb, tt, final):
    b, t, d = x.shape
    dd, dc = o.shape[1], u.shape[1]
    nt = t // tt
    tm = bb * tt
    kern = functools.partial(_outproj_kernel, final=final)
    return pl.pallas_call(
        kern,
        out_shape=jax.ShapeDtypeStruct((b, t, d), F32),
        grid=(b // bb, nt),
        in_specs=[pl.BlockSpec((bb, tt, d), lambda i, j: (i, j, 0)),
                  pl.BlockSpec((bb, 1, d), lambda i, j: (i, 0, 2)),
                  pl.BlockSpec((tm, dd), lambda i, j: (i * nt + j, 0)),
                  pl.BlockSpec((tm, dc), lambda i, j: (i * nt + j, 0)),
                  pl.BlockSpec((1, dd, d), lambda i, j: (layer, 0, 0)),
                  pl.BlockSpec((1, dc, d), lambda i, j: (layer, dd // dc, 0)),
                  pl.BlockSpec((1, d), lambda i, j: (0, 0))],
        out_specs=pl.BlockSpec((bb, tt, d), lambda i, j: (i, j, 0)),
        compiler_params=_params(2),
        name="out_proj",
    )(x, mod, o, u, w_out, w_out, final_g.reshape(1, d))


def kernel(x_prompt, x_sample, c_prompt, c_sample, state_delta, state_qkv_conv, state_glu_conv,
           norm_g, w_ada, b_ada, w_in, w_qkv_conv, a_log, dt_bias, head_norm_g, w_dw, b_dw,
           ln_g, ln_b, w_out, final_g):
    depth = w_in.shape[0]
    bp, tp, d = x_prompt.shape
    bs, ts, _ = x_sample.shape
    heads = a_log.shape[1]
    dd = w_qkv_conv.shape[2] // 3
    dk = dd // heads
    dc = w_dw.shape[2]
    assert ts % SUBLANE == 0 and ts >= QKV_CONV - 1 and ts <= CONF_KERNEL - 1
    assert 2 * heads <= LANE and dd % dc == 0 and tp % CHUNK == 0
    assert CHUNK % ts == 0 and bs % (CHUNK // ts) == 0 and (CHUNK // ts) % 2 == 0

    mod = _ada(jnp.concatenate([c_prompt, c_sample], axis=0), w_ada, b_ada)

    tn = min(1024, dd, dc)
    tm = 1024
    tt_in = min(tm, tp)
    bb_s = min(bs, tm // ts)
    tt_out = min(512, tp)
    bb_out = min(bs, 512 // ts)
    tt_conf = min(512, tp)
    bb_conf = min(bs, 16)
    nb_delta = 4 if bp % 4 == 0 else 1
    nseq = CHUNK // ts

    w_t = jnp.swapaxes(w_in, 1, 2)
    w_ba = jnp.pad(w_t[:, 4 * dd:4 * dd + 2 * heads], ((0, 0), (0, LANE - 2 * heads), (0, 0)))
    w_out_b = w_out.astype(BF16)
    qkv_hist = jnp.swapaxes(state_qkv_conv, 1, 2)
    glu_hist = jnp.swapaxes(state_glu_conv, 1, 2)
    norm_g3 = norm_g.reshape(depth, 1, d)
    alog = jnp.pad(a_log, ((0, 0), (heads, LANE - 2 * heads))).reshape(depth, 1, LANE)
    dtb = jnp.pad(dt_bias, ((0, 0), (heads, LANE - 2 * heads))).reshape(depth, 1, LANE)
    hng = head_norm_g.reshape(depth, 1, dk)
    bdw, lng, lnb = b_dw.reshape(depth, 1, dc), ln_g.reshape(depth, 1, dc), ln_b.reshape(depth, 1, dc)

    hp, hs = x_prompt, x_sample
    sp_l, qp_l, gp_l = [], [], []
    s_s = q_s = g_s = None
    for l in range(depth):
        mod_p = mod[l, :bp].reshape(bp, 1, 3 * d)
        mod_s = mod[l, bp:].reshape(bs, 1, 3 * d)
        final = l == depth - 1

        proj, ba = _inproj(hp, mod_p, norm_g3, w_t, w_ba, l, bb=1, tt=tt_in, tn=tn,
                           n_a=4 * dd // tn, skip=2 * heads)
        o, s_new, qbuf = _delta_prompt(proj, ba, w_qkv_conv, alog, dtb, hng, l,
                                       b=bp, t=tp, nb=nb_delta, heads=heads, dk=dk)
        u, gbuf = _conf_prompt(proj, w_dw, bdw, lng, lnb, l, b=bp, t=tp, dd=dd, dc=dc, tt=tt_conf)
        hp = _outproj(hp, mod_p, o, u, w_out_b, final_g, l, bb=1, tt=tt_out, final=final)
        sp_l.append(s_new); qp_l.append(qbuf); gp_l.append(gbuf)

        proj, ba = _inproj(hs, mod_s, norm_g3, w_t, w_ba, l, bb=bb_s, tt=ts, tn=tn,
                           n_a=4 * dd // tn, skip=2 * heads)
        o, s_s, q_s = _delta_sample(proj, ba, qkv_hist, state_delta,
                                    None if l == 0 else (s_s, q_s), l, w_qkv_conv, alog, dtb, hng,
                                    b=bs, t=ts, nseq=nseq, heads=heads, dk=dk)
        u, g_s = _conf_sample(proj, glu_hist, g_s, l, w_dw, bdw, lng, lnb,
                              b=bs, t=ts, dd=dd, dc=dc, bb=bb_conf)
        hs = _outproj(hs, mod_s, o, u, w_out_b, final_g, l, bb=bb_out, tt=ts, final=final)

    return (hp, hs, jnp.stack(sp_l), jnp.stack(qp_l), jnp.stack(gp_l),
            s_s, jnp.swapaxes(q_s, 1, 2), jnp.swapaxes(g_s, 1, 2))
```

```python
import functools
import math

import jax
import jax.numpy as jnp
from jax import lax
from jax.experimental import pallas as pl
from jax.experimental.pallas import tpu as pltpu

F32 = jnp.float32
BF16 = jnp.bfloat16

EPS = 1e-6
QKV_CONV = 4
CONF_KERNEL = 31
CHUNK = 64
LANE = 128
SUBLANE = 8
QKV_HIST = SUBLANE
GLU_HIST = 32
CONF_PARTIALS = 3
VMEM_LIMIT_BYTES = 56 * 1024 * 1024


def _params(n_axes):
    return pltpu.CompilerParams(dimension_semantics=("arbitrary",) * n_axes,
                                vmem_limit_bytes=VMEM_LIMIT_BYTES)


def _sigmoid(x):
    return 0.5 * jnp.tanh(0.5 * x) + 0.5


def _glu(a, b):
    half = 0.5 * a
    return half * jnp.tanh(0.5 * b) + half


def _silu(x):
    half = 0.5 * x
    return half * jnp.tanh(half) + half


def _softplus(x):
    return jnp.maximum(x, 0.0) + jnp.log1p(jnp.exp(-jnp.abs(x)))


def _dot(a, b):
    return jnp.dot(a, b, preferred_element_type=F32)


def _dot_nt(a, b):
    return lax.dot_general(a, b, (((1,), (1,)), ((), ())), preferred_element_type=F32)


def _dot_tn(a, b):
    return lax.dot_general(a, b, (((0,), (0,)), ((), ())), preferred_element_type=F32)


def _ada_kernel(c_ref, w_ref, b_ref, o_ref):
    s = _silu(c_ref[...]).astype(BF16)
    o_ref[0] = _dot(s, w_ref[0].astype(BF16)) + b_ref[0]


def _ada(c_all, w_ada, b_ada):
    depth, d, n = w_ada.shape
    rows = c_all.shape[0]
    tn = n // 4 if n % (4 * LANE) == 0 else n
    return pl.pallas_call(
        _ada_kernel,
        out_shape=jax.ShapeDtypeStruct((depth, rows, n), F32),
        grid=(depth, n // tn),
        in_specs=[pl.BlockSpec((rows, d), lambda l, j: (0, 0)),
                  pl.BlockSpec((1, d, tn), lambda l, j: (l, 0, j)),
                  pl.BlockSpec((1, 1, tn), lambda l, j: (l, 0, j))],
        out_specs=pl.BlockSpec((1, rows, tn), lambda l, j: (l, 0, j)),
        compiler_params=_params(2),
        name="ada_mod",
    )(c_all, w_ada, b_ada.reshape(depth, 1, n))


def _inproj_kernel(x_ref, shift_ref, scale_ref, g_ref, w_ref, wba_ref, proj_ref, ba_ref, h_ref):
    @pl.when(pl.program_id(2) == 0)
    def _():
        x = x_ref[...]
        ms = jnp.mean(x * x, axis=-1, keepdims=True)
        gain = g_ref[0] * (1.0 + scale_ref[...])
        h = x * lax.rsqrt(ms + EPS) * gain + shift_ref[...]
        h = h.reshape(h_ref.shape).astype(BF16)
        h_ref[...] = h
        ba_ref[...] = _dot_nt(h, wba_ref[0].astype(BF16))

    proj_ref[...] = _dot_nt(h_ref[...], w_ref[0].astype(BF16))


def _inproj(x, mod, norm_g, w_t, w_ba, layer, *, bb, tt, tn, n_a, skip):
    b, t, d = x.shape
    n = w_t.shape[1] - skip
    nb, nt, nn = b // bb, t // tt, n // tn
    tm = bb * tt
    return pl.pallas_call(
        _inproj_kernel,
        out_shape=(jax.ShapeDtypeStruct((b * t, n), F32),
                   jax.ShapeDtypeStruct((b * t, LANE), F32)),
        grid=(nb, nt, nn),
        in_specs=[pl.BlockSpec((bb, tt, d), lambda i, j, k: (i, j, 0)),
                  pl.BlockSpec((bb, 1, d), lambda i, j, k: (i, 0, 0)),
                  pl.BlockSpec((bb, 1, d), lambda i, j, k: (i, 0, 1)),
                  pl.BlockSpec((1, 1, d), lambda i, j, k: (layer, 0, 0)),
                  pl.BlockSpec((pl.Element(1), pl.Element(tn), pl.Element(d)),
                               lambda i, j, k: (layer, pl.multiple_of(
                                   k * tn + jnp.where(k >= n_a, skip, 0), math.gcd(tn, skip)), 0)),
                  pl.BlockSpec((1, LANE, d), lambda i, j, k: (layer, 0, 0))],
        out_specs=(pl.BlockSpec((tm, tn), lambda i, j, k: (i * nt + j, k)),
                   pl.BlockSpec((tm, LANE), lambda i, j, k: (i * nt + j, 0))),
        scratch_shapes=[pltpu.VMEM((tm, d), BF16)],
        compiler_params=_params(3),
        name="in_proj",
    )(x, mod, mod, norm_g, w_t, w_ba)


def _short_conv(tiles, pick_cur, pick_prev, wconv_ref):
    cur, prev = pick_cur(tiles), pick_prev(tiles)
    sub = lax.broadcasted_iota(jnp.int32, cur.shape, cur.ndim - 2)
    acc = wconv_ref[0, QKV_CONV - 1:QKV_CONV, :] * cur
    for d in range(1, QKV_CONV):
        delayed = pltpu.roll(jnp.where(sub < SUBLANE - d, cur, prev), d, axis=1)
        acc = acc + wconv_ref[0, QKV_CONV - 1 - d:QKV_CONV - d, :] * delayed
    return acc


def _gates(ba, alog, dtb):
    beta = _sigmoid(ba)
    g = -jnp.exp(alog) * _softplus(ba + dtb)
    return beta, g


def _delta_rows(qkv, z_ref, ba, alog, dtb, hng, s_prev, *, c, nseq, heads, dk):
    rows_n = nseq * c
    dd = heads * dk
    hs = range(heads)
    rows = lax.broadcasted_iota(jnp.int32, (rows_n, rows_n), 0)
    cols = lax.broadcasted_iota(jnp.int32, (rows_n, rows_n), 1)
    if nseq > 1:
        shift = c.bit_length() - 1
        same = (rows >> shift) == (cols >> shift)
        causal = same & (rows >= cols)
        strict = same & (rows > cols)
        upper = same & (rows <= cols)
        ones = jnp.where(same, 1.0, 0.0).astype(BF16)
    else:
        causal = rows >= cols
        strict = rows > cols
        upper = rows <= cols
        ones = jnp.ones((rows_n, rows_n), BF16)
    lower_b = jnp.where(causal, 1.0, 0.0).astype(BF16)
    upper_b = jnp.where(upper, 1.0, 0.0).astype(BF16)
    eye = jnp.where(rows == cols, 1.0, 0.0)

    beta, g = _gates(ba, alog, dtb)
    g0 = g.astype(BF16)
    r1 = g - g0.astype(F32)
    g1 = r1.astype(BF16)
    g2 = (r1 - g1.astype(F32)).astype(BF16)
    gc = _dot(lower_b, g0) + (_dot(lower_b, g1) + _dot(lower_b, g2))
    gct = _dot_tn(g0, upper_b) + (_dot_tn(g1, upper_b) + _dot_tn(g2, upper_b))
    gl = _dot(ones, g0) + (_dot(ones, g1) + _dot(ones, g2))
    yield

    def l2n(x, scale):
        return x * (lax.rsqrt(jnp.sum(x * x, axis=-1, keepdims=True) + EPS) * scale)

    qn = [l2n(qkv[:, h * dk:(h + 1) * dk], dk ** -0.5) for h in hs]
    kn = [l2n(qkv[:, dd + h * dk:dd + (h + 1) * dk], 1.0) for h in hs]
    vv = [qkv[:, 2 * dd + h * dk:2 * dd + (h + 1) * dk] for h in hs]
    bcol = [beta[:, h:h + 1] for h in hs]
    gcol = [gc[:, heads + h:heads + h + 1] for h in hs]
    glc = [gl[:, heads + h:heads + h + 1] for h in hs]
    decay = [jnp.where(causal, jnp.exp(jnp.where(causal, gcol[h] - gct[heads + h:heads + h + 1, :], 0.0)), 0.0)
             for h in hs]
    kb = [kn[h].astype(BF16) for h in hs]
    kq = [jnp.concatenate([kn[h], qn[h]], axis=0).astype(BF16) for h in hs]
    kkqk = [_dot_nt(kq[h], kb[h]) for h in hs]
    yield
    low = [jnp.where(strict, bcol[h] * kkqk[h][:rows_n] * decay[h], 0.0) for h in hs]
    attn = [(kkqk[h][rows_n:] * decay[h]).astype(BF16) for h in hs]
    inv = [eye - jnp.where((rows ^ cols) == 1, low[h], 0.0) for h in hs]
    b = 2
    while b < c:
        pair = ((rows ^ cols) >= b) & ((rows ^ cols) < 2 * b)
        off = [jnp.where(pair, low[h], 0.0).astype(BF16) for h in hs]
        invb = [inv[h].astype(BF16) for h in hs]
        y = [_dot(off[h], invb[h]).astype(BF16) for h in hs]
        yield
        inv = [inv[h] - _dot(invb[h], y[h]) for h in hs]
        yield
        b *= 2
    egc = [jnp.exp(gcol[h]) for h in hs]
    rhs = [jnp.concatenate([bcol[h] * vv[h], (bcol[h] * egc[h]) * kn[h]], axis=1).astype(BF16) for h in hs]
    sol = [_dot(inv[h].astype(BF16), rhs[h]) for h in hs]
    yield
    qg = [qn[h] * egc[h] for h in hs]
    kdb = [(kn[h] * jnp.exp(glc[h] - gcol[h])).astype(BF16) for h in hs]

    if nseq == 1:
        sb = [s_prev[0][h].astype(BF16) for h in hs]
        wq = [jnp.concatenate([sol[h][:, dk:], qg[h]], axis=0).astype(BF16) for h in hs]
        ys = [_dot(wq[h], sb[h]) for h in hs]
        yield
        ub = [(sol[h][:, :dk] - ys[h][:rows_n]).astype(BF16) for h in hs]
        o = [ys[h][rows_n:] + _dot(attn[h], ub[h]) for h in hs]
        yield
        s_new = [[jnp.exp(glc[h][0:1, :]) * s_prev[0][h] + _dot_tn(kdb[h], ub[h]) for h in hs]]
    else:
        js = range(nseq)
        sb = [[s_prev[j][h].astype(BF16) for h in hs] for j in js]
        wq = [[jnp.concatenate([sol[h][j * c:(j + 1) * c, dk:], qg[h][j * c:(j + 1) * c]], axis=0).astype(BF16)
               for h in hs] for j in js]
        ys = [[_dot(wq[j][h], sb[j][h]) for h in hs] for j in js]
        ws = [jnp.concatenate([ys[j][h][:c] for j in js], axis=0) for h in hs]
        qs = [jnp.concatenate([ys[j][h][c:] for j in js], axis=0) for h in hs]
        u = [sol[h][:, :dk] - ws[h] for h in hs]
        o = [qs[h] + _dot(attn[h], u[h].astype(BF16)) for h in hs]
        seq_of_row = lax.broadcasted_iota(jnp.int32, (rows_n, dk), 0) >> (c.bit_length() - 1)
        s_new = [[None] * heads for _ in js]
        for j in range(0, nseq, 2):
            um = [jnp.concatenate([jnp.where(seq_of_row == j, u[h], 0.0),
                                   jnp.where(seq_of_row == j + 1, u[h], 0.0)], axis=1).astype(BF16) for h in hs]
            sn = [_dot_tn(kdb[h], um[h]) for h in hs]
            for h in hs:
                s_new[j][h] = jnp.exp(glc[h][j * c:j * c + 1, :]) * s_prev[j][h] + sn[h][:, :dk]
                s_new[j + 1][h] = (jnp.exp(glc[h][(j + 1) * c:(j + 1) * c + 1, :]) * s_prev[j + 1][h]
                                   + sn[h][:, dk:])
    out = []
    for h in hs:
        on = o[h] * lax.rsqrt(jnp.mean(o[h] * o[h], axis=-1, keepdims=True) + EPS) * hng
        out.append(on * _silu(z_ref[:, h * dk:(h + 1) * dk]))
    return out, s_new


def _lockstep(gens):
    results = [None] * len(gens)
    live = list(range(len(gens)))
    while live:
        for idx in list(live):
            try:
                next(gens[idx])
            except StopIteration as stop:
                results[idx] = stop.value
                live.remove(idx)
    return results


def _delta_prompt_kernel(qkv_ref, z_ref, ba_ref, wconv_ref, alog_ref, dtb_ref, hng_ref,
                         o_ref, s_ref, qbuf_ref, ext_ref, *, c, nb, heads, dk):
    i = pl.program_id(1)

    @pl.when(i == 0)
    def _():
        ext_ref[...] = jnp.zeros(ext_ref.shape, F32)
        s_ref[...] = jnp.zeros(s_ref.shape, F32)

    gens = []
    for g in range(nb):
        xe = jnp.concatenate([ext_ref[g], qkv_ref[g]], axis=0)
        tiles = xe.reshape((QKV_HIST + c) // SUBLANE, SUBLANE, xe.shape[1])
        acc = _short_conv(tiles, lambda a: a[1:], lambda a: a[:-1], wconv_ref).reshape(c, xe.shape[1])
        ext_ref[g] = qkv_ref[g, c - QKV_HIST:c, :]
        s_prev = [[s_ref[g, h] for h in range(heads)]]
        gens.append(_delta_rows(_silu(acc), z_ref.at[g], ba_ref[g], alog_ref[0], dtb_ref[0], hng_ref[0],
                                s_prev, c=c, nseq=1, heads=heads, dk=dk))
    for g, (out, s_new) in enumerate(_lockstep(gens)):
        for h in range(heads):
            o_ref[g, :, h * dk:(h + 1) * dk] = out[h].astype(o_ref.dtype)
            s_ref[g, h] = s_new[0][h]

    @pl.when(i == pl.num_programs(1) - 1)
    def _():
        qbuf_ref[...] = qkv_ref[:, c - (QKV_CONV - 1):c, :]


def _layer_spec(arr, layer):
    zeros = (0,) * (arr.ndim - 1)
    return pl.BlockSpec((1,) + arr.shape[1:], lambda *_: (layer,) + zeros)


def _delta_prompt(proj, ba, wconv, alog, dtb, hng, layer, *, b, t, nb, heads, dk):
    dd = heads * dk
    c = min(CHUNK, t)
    nc = t // c
    proj3 = proj.reshape(b, t, proj.shape[1])
    kern = functools.partial(_delta_prompt_kernel, c=c, nb=nb, heads=heads, dk=dk)
    o, s_new, qbuf = pl.pallas_call(
        kern,
        out_shape=(jax.ShapeDtypeStruct((b, t, dd), BF16),
                   jax.ShapeDtypeStruct((b, heads, dk, dk), F32),
                   jax.ShapeDtypeStruct((b, QKV_CONV - 1, 3 * dd), F32)),
        grid=(b // nb, nc),
        in_specs=[pl.BlockSpec((nb, c, 3 * dd), lambda i, j: (i, j, 0)),
                  pl.BlockSpec((nb, c, dd), lambda i, j: (i, j, 3)),
                  pl.BlockSpec((nb, c, LANE), lambda i, j: (i, j, 0)),
                  _layer_spec(wconv, layer), _layer_spec(alog, layer), _layer_spec(dtb, layer),
                  _layer_spec(hng, layer)],
        out_specs=(pl.BlockSpec((nb, c, dd), lambda i, j: (i, j, 0)),
                   pl.BlockSpec((nb, heads, dk, dk), lambda i, j: (i, 0, 0, 0)),
                   pl.BlockSpec((nb, QKV_CONV - 1, 3 * dd), lambda i, j: (i, 0, 0))),
        scratch_shapes=[pltpu.VMEM((nb, QKV_HIST, 3 * dd), F32)],
        compiler_params=_params(2),
        name="delta_prompt",
    )(proj3, proj3, ba.reshape(b, t, LANE), wconv, alog, dtb, hng)
    return o.reshape(b * t, dd), s_new, qbuf


def _write_slab(ref, slab, value):
    for other in range(ref.shape[0]):
        if other != slab:
            ref[other] = jnp.zeros(ref.shape[1:], ref.dtype)
    ref[slab] = value


def _delta_sample_kernel(qkv_ref, z_ref, ba_ref, hist_ref, s0_ref, wconv_ref, alog_ref, dtb_ref,
                         hng_ref, *rest, t, nseq, heads, dk, slab):
    o_ref, s_ref, qbuf_ref, ext_ref = rest[-4:]
    n = ext_ref.shape[2]
    first = QKV_HIST - (QKV_CONV - 1)
    ext_ref[:, 0:first, :] = jnp.zeros((nseq, first, n), F32)
    for r in range(QKV_CONV - 1):
        ext_ref[:, first + r, :] = hist_ref[0, r]
    ext_ref[:, QKV_HIST:QKV_HIST + t, :] = qkv_ref[...].reshape(nseq, t, n)
    per_seq = (QKV_HIST + t) // SUBLANE
    tiles = ext_ref[...].reshape(nseq * per_seq, SUBLANE, n)

    def pick(a, lo, hi):
        return a.reshape(nseq, per_seq, SUBLANE, n)[:, lo:hi].reshape(nseq * (per_seq - 1), SUBLANE, n)

    acc = _short_conv(tiles, lambda a: pick(a, 1, per_seq), lambda a: pick(a, 0, per_seq - 1),
                      wconv_ref).reshape(nseq * t, n)
    _write_slab(qbuf_ref, slab, jnp.stack(
        [ext_ref[:, QKV_HIST + t - (QKV_CONV - 1) + r, :] for r in range(QKV_CONV - 1)]))

    s_prev = [[s0_ref[0, j, h] for h in range(heads)] for j in range(nseq)]
    (out, s_new), = _lockstep([_delta_rows(_silu(acc), z_ref, ba_ref[...], alog_ref[0], dtb_ref[0],
                                           hng_ref[0], s_prev, c=t, nseq=nseq, heads=heads, dk=dk)])
    for other in range(s_ref.shape[0]):
        if other != slab:
            s_ref[other] = jnp.zeros(s_ref.shape[1:], F32)
    for h in range(heads):
        o_ref[:, h * dk:(h + 1) * dk] = out[h].astype(o_ref.dtype)
        for j in range(nseq):
            s_ref[slab, j, h] = s_new[j][h]


def _layered_out(depth, layer, prev, block_tail, n_in, batch_pos=0):
    def index(first, i):
        tail = [0] * len(block_tail)
        tail[batch_pos] = i
        return (first,) + tuple(tail)

    if prev is None:
        return pl.BlockSpec((depth,) + block_tail, lambda i: index(0, i)), layer, None
    return pl.BlockSpec((1,) + block_tail, lambda i: index(layer, i)), 0, n_in


def _delta_sample(proj, ba, hist, state, prev, layer, wconv, alog, dtb, hng, *, b, t, nseq, heads, dk):
    dd = heads * dk
    depth = state.shape[0]
    rows_n = nseq * t
    in_specs = [pl.BlockSpec((rows_n, 3 * dd), lambda i: (i, 0)),
                pl.BlockSpec((rows_n, dd), lambda i: (i, 3)),
                pl.BlockSpec((rows_n, LANE), lambda i: (i, 0)),
                pl.BlockSpec((1, QKV_CONV - 1, nseq, 3 * dd), lambda i: (layer, 0, i, 0)),
                pl.BlockSpec((1, nseq, heads, dk, dk), lambda i: (layer, i, 0, 0, 0)),
                _layer_spec(wconv, layer), _layer_spec(alog, layer), _layer_spec(dtb, layer),
                _layer_spec(hng, layer)]
    args = [proj, proj, ba, hist, state, wconv, alog, dtb, hng]
    s_spec, slab, alias_at = _layered_out(depth, layer, prev, (nseq, heads, dk, dk), len(args))
    q_spec, _, _ = _layered_out(depth, layer, prev, (QKV_CONV - 1, nseq, 3 * dd), len(args), batch_pos=1)
    aliases = {}
    if prev is not None:
        in_specs += [pl.BlockSpec(memory_space=pl.ANY)] * 2
        args += list(prev)
        aliases = {alias_at: 1, alias_at + 1: 2}
    kern = functools.partial(_delta_sample_kernel, t=t, nseq=nseq, heads=heads, dk=dk, slab=slab)
    return pl.pallas_call(
        kern,
        out_shape=(jax.ShapeDtypeStruct((b * t, dd), BF16),
                   jax.ShapeDtypeStruct((depth, b, heads, dk, dk), F32),
                   jax.ShapeDtypeStruct((depth, QKV_CONV - 1, b, 3 * dd), F32)),
        grid=(b // nseq,),
        in_specs=in_specs,
        out_specs=(pl.BlockSpec((rows_n, dd), lambda i: (i, 0)), s_spec, q_spec),
        scratch_shapes=[pltpu.VMEM((nseq, QKV_HIST + t, 3 * dd), F32)],
        input_output_aliases=aliases,
        compiler_params=_params(1),
        name="delta_sample",
    )(*args)


def _conf_epilogue(conv, zc, bdw, lng, lnb):
    y = conv + bdw
    mu = jnp.mean(y, axis=-1, keepdims=True)
    yc = y - mu
    var = jnp.mean(yc * yc, axis=-1, keepdims=True)
    y = yc * lax.rsqrt(var + EPS) * lng + lnb
    return _silu(y) * _silu(zc)


def _conf_tile(ga_ref, gb_ref, zc_ref, w_ref, bdw_ref, lng_ref, lnb_ref, ext_ref, sh_ref, conv_ref,
               *, tt, rr):
    dc = ext_ref.shape[1]
    first = GLU_HIST - (CONF_KERNEL - 1)
    ext_ref[GLU_HIST:GLU_HIST + tt, :] = _glu(ga_ref[...], gb_ref[...])
    for p in range(SUBLANE):
        for lt in range(dc // LANE):
            sh_ref[p, lt] = ext_ref[p:p + GLU_HIST + tt, lt * LANE:(lt + 1) * LANE]
    for lt in range(dc // LANE):
        lanes = slice(lt * LANE, (lt + 1) * LANE)
        taps = [jnp.broadcast_to(w_ref[0, j:j + 1, lanes], (rr, LANE)) for j in range(CONF_KERNEL)]

        def body(r, carry, lt=lt, lanes=lanes, taps=taps):
            r0 = pl.multiple_of(r * rr, rr)
            parts = [None] * CONF_PARTIALS
            for j in range(CONF_KERNEL):
                off = first + j
                win = sh_ref[off % SUBLANE, lt, pl.ds(r0 + (off // SUBLANE) * SUBLANE, rr), :]
                p = j % CONF_PARTIALS
                parts[p] = taps[j] * win if parts[p] is None else parts[p] + taps[j] * win
            acc = parts[0]
            for p in range(1, CONF_PARTIALS):
                acc = acc + parts[p]
            conv_ref[pl.ds(r0, rr), lanes] = acc
            return carry

        lax.fori_loop(0, tt // rr, body, 0)
    y = _conf_epilogue(conv_ref[...], zc_ref[...], bdw_ref[0], lng_ref[0], lnb_ref[0])
    ext_ref[0:GLU_HIST, :] = ext_ref[tt:tt + GLU_HIST, :]
    return y


def _conf_prompt_kernel(ga_ref, gb_ref, zc_ref, w_ref, bdw_ref, lng_ref, lnb_ref,
                        u_ref, gbuf_ref, ext_ref, sh_ref, conv_ref, *, tt, rr):
    i = pl.program_id(1)
    dc = ext_ref.shape[1]

    @pl.when(i == 0)
    def _():
        ext_ref[0:GLU_HIST, :] = jnp.zeros((GLU_HIST, dc), F32)
        ext_ref[GLU_HIST + tt:GLU_HIST + tt + SUBLANE, :] = jnp.zeros((SUBLANE, dc), F32)

    y = _conf_tile(ga_ref, gb_ref, zc_ref, w_ref, bdw_ref, lng_ref, lnb_ref, ext_ref, sh_ref, conv_ref,
                   tt=tt, rr=rr)
    u_ref[...] = y.astype(u_ref.dtype)

    @pl.when(i == pl.num_programs(1) - 1)
    def _():
        gbuf_ref[0] = ext_ref[GLU_HIST - (CONF_KERNEL - 1):GLU_HIST, :]


def _conf_prompt(proj, w_dw, b_dw, ln_g, ln_b, layer, *, b, t, dd, dc, tt):
    nt = t // tt
    rr = min(32, tt)
    col0 = (4 * dd) // dc
    kern = functools.partial(_conf_prompt_kernel, tt=tt, rr=rr)
    row = lambda i, j: i * nt + j
    return pl.pallas_call(
        kern,
        out_shape=(jax.ShapeDtypeStruct((b * t, dc), BF16),
                   jax.ShapeDtypeStruct((b, CONF_KERNEL - 1, dc), F32)),
        grid=(b, nt),
        in_specs=[pl.BlockSpec((tt, dc), lambda i, j: (row(i, j), col0)),
                  pl.BlockSpec((tt, dc), lambda i, j: (row(i, j), col0 + 1)),
                  pl.BlockSpec((tt, dc), lambda i, j: (row(i, j), col0 + 2)),
                  _layer_spec(w_dw, layer), _layer_spec(b_dw, layer), _layer_spec(ln_g, layer),
                  _layer_spec(ln_b, layer)],
        out_specs=(pl.BlockSpec((tt, dc), lambda i, j: (row(i, j), 0)),
                   pl.BlockSpec((1, CONF_KERNEL - 1, dc), lambda i, j: (i, 0, 0))),
        scratch_shapes=[pltpu.VMEM((GLU_HIST + tt + SUBLANE, dc), F32),
                        pltpu.VMEM((SUBLANE, dc // LANE, GLU_HIST + tt, LANE), F32),
                        pltpu.VMEM((tt, dc), F32)],
        compiler_params=_params(2),
        name="conf_prompt",
    )(proj, proj, proj, w_dw, b_dw, ln_g, ln_b)


def _conf_sample_kernel(ga_ref, gb_ref, zc_ref, hist_ref, w_ref, bdw_ref, lng_ref, lnb_ref,
                        *rest, t, bb, slab):
    u_ref, gbuf_ref, ext_ref, sh_ref, conv_ref = rest[-5:]
    dc = ext_ref.shape[2]
    first = GLU_HIST - (CONF_KERNEL - 1)
    rows = GLU_HIST + t
    ext_ref[:, 0:first, :] = jnp.zeros((bb, first, dc), F32)
    for r in range(CONF_KERNEL - 1):
        ext_ref[:, first + r, :] = hist_ref[0, r]
    ext_ref[:, GLU_HIST:rows, :] = _glu(ga_ref[...], gb_ref[...]).reshape(bb, t, dc)
    ext_ref[:, rows:rows + SUBLANE, :] = jnp.zeros((bb, SUBLANE, dc), F32)
    for p in range(SUBLANE):
        for lt in range(dc // LANE):
            sh_ref[p, lt] = ext_ref[:, p:p + rows, lt * LANE:(lt + 1) * LANE]
    for lt in range(dc // LANE):
        lanes = slice(lt * LANE, (lt + 1) * LANE)
        acc = None
        for j in range(CONF_KERNEL):
            off = first + j
            base = (off // SUBLANE) * SUBLANE
            term = w_ref[0, j:j + 1, lanes] * sh_ref[off % SUBLANE, lt, :, base:base + t, :]
            acc = term if acc is None else acc + term
        conv_ref[:, lanes] = acc.reshape(bb * t, LANE)
    y = _conf_epilogue(conv_ref[...], zc_ref[...], bdw_ref[0], lng_ref[0], lnb_ref[0])
    u_ref[...] = y.astype(u_ref.dtype)
    _write_slab(gbuf_ref, slab, jnp.stack([ext_ref[:, first + t + r, :] for r in range(CONF_KERNEL - 1)]))


def _conf_sample(proj, hist, prev, layer, w_dw, b_dw, ln_g, ln_b, *, b, t, dd, dc, bb):
    depth = hist.shape[0]
    col0 = (4 * dd) // dc
    in_specs = [pl.BlockSpec((bb * t, dc), lambda i: (i, col0)),
                pl.BlockSpec((bb * t, dc), lambda i: (i, col0 + 1)),
                pl.BlockSpec((bb * t, dc), lambda i: (i, col0 + 2)),
                pl.BlockSpec((1, CONF_KERNEL - 1, bb, dc), lambda i: (layer, 0, i, 0)),
                _layer_spec(w_dw, layer), _layer_spec(b_dw, layer), _layer_spec(ln_g, layer),
                _layer_spec(ln_b, layer)]
    args = [proj, proj, proj, hist, w_dw, b_dw, ln_g, ln_b]
    g_spec, slab, alias_at = _layered_out(depth, layer, prev, (CONF_KERNEL - 1, bb, dc), len(args),
                                          batch_pos=1)
    aliases = {}
    if prev is not None:
        in_specs.append(pl.BlockSpec(memory_space=pl.ANY))
        args.append(prev)
        aliases = {alias_at: 1}
    kern = functools.partial(_conf_sample_kernel, t=t, bb=bb, slab=slab)
    return pl.pallas_call(
        kern,
        out_shape=(jax.ShapeDtypeStruct((b * t, dc), BF16),
                   jax.ShapeDtypeStruct((depth, CONF_KERNEL - 1, b, dc), F32)),
        grid=(b // bb,),
        in_specs=in_specs,
        out_specs=(pl.BlockSpec((bb * t, dc), lambda i: (i, 0)), g_spec),
        scratch_shapes=[pltpu.VMEM((bb, GLU_HIST + t + SUBLANE, dc), F32),
                        pltpu.VMEM((SUBLANE, dc // LANE, bb, GLU_HIST + t, LANE), F32),
                        pltpu.VMEM((bb * t, dc), F32)],
        input_output_aliases=aliases,
        compiler_params=_params(1),
        name="conf_sample",
    )(*args)


def _outproj_kernel(x_ref, gate_ref, o_ref, u_ref, wo_ref, wu_ref, fg_ref, y_ref, *, final):
    mix = _dot(o_ref[...], wo_ref[0]) + _dot(u_ref[...], wu_ref[0])
    x = x_ref[...]
    y = x + gate_ref[...] * mix.reshape(x.shape)
    if final:
        y = y * lax.rsqrt(jnp.mean(y * y, axis=-1, keepdims=True) + EPS) * fg_ref[...]
    y_ref[...] = y


def _outproj(x, mod, o, u, w_out, final_g, layer, *, bb, tt, final):
    b, t, d = x.shape
    dd, dc = o.shape[1], u.shape[1]
    nt = t // tt
    tm = bb * tt
    kern = functools.partial(_outproj_kernel, final=final)
    return pl.pallas_call(
        kern,
        out_shape=jax.ShapeDtypeStruct((b, t, d), F32),
        grid=(b // bb, nt),
        in_specs=[pl.BlockSpec((bb, tt, d), lambda i, j: (i, j, 0)),
                  pl.BlockSpec((bb, 1, d), lambda i, j: (i, 0, 2)),
                  pl.BlockSpec((tm, dd), lambda i, j: (i * nt + j, 0)),
                  pl.BlockSpec((tm, dc), lambda i, j: (i * nt + j, 0)),
                  pl.BlockSpec((1, dd, d), lambda i, j: (layer, 0, 0)),
                  pl.BlockSpec((1, dc, d), lambda i, j: (layer, dd // dc, 0)),
                  pl.BlockSpec((1, d), lambda i, j: (0, 0))],
        out_specs=pl.BlockSpec((bb, tt, d), lambda i, j: (i, j, 0)),
        compiler_params=_params(2),
        name="out_proj",
    )(x, mod, o, u, w_out, w_out, final_g.reshape(1, d))


def kernel(x_prompt, x_sample, c_prompt, c_sample, state_delta, state_qkv_conv, state_glu_conv,
           norm_g, w_ada, b_ada, w_in, w_qkv_conv, a_log, dt_bias, head_norm_g, w_dw, b_dw,
           ln_g, ln_b, w_out, final_g):
    depth = w_in.shape[0]
    bp, tp, d = x_prompt.shape
    bs, ts, _ = x_sample.shape
    heads = a_log.shape[1]
    dd = w_qkv_conv.shape[2] // 3
    dk = dd // heads
    dc = w_dw.shape[2]
    assert ts % SUBLANE == 0 and ts >= QKV_CONV - 1 and ts <= CONF_KERNEL - 1
    assert 2 * heads <= LANE and dd % dc == 0 and tp % CHUNK == 0
    assert CHUNK % ts == 0 and bs % (CHUNK // ts) == 0 and (CHUNK // ts) % 2 == 0

    mod = _ada(jnp.concatenate([c_prompt, c_sample], axis=0), w_ada, b_ada)

    tn = min(1024, dd, dc)
    tm = 1024
    tt_in = min(tm, tp)
    bb_s = min(bs, tm // ts)
    tt_out = min(512, tp)
    bb_out = min(bs, 512 // ts)
    tt_conf = min(256, tp)
    bb_conf = min(bs, 16)
    nb_delta = 4 if bp % 4 == 0 else 1
    nseq = CHUNK // ts

    w_t = jnp.swapaxes(w_in, 1, 2)
    w_ba = jnp.pad(w_t[:, 4 * dd:4 * dd + 2 * heads], ((0, 0), (0, LANE - 2 * heads), (0, 0)))
    w_out_b = w_out.astype(BF16)
    qkv_hist = jnp.swapaxes(state_qkv_conv, 1, 2)
    glu_hist = jnp.swapaxes(state_glu_conv, 1, 2)
    norm_g3 = norm_g.reshape(depth, 1, d)
    alog = jnp.pad(a_log, ((0, 0), (heads, LANE - 2 * heads))).reshape(depth, 1, LANE)
    dtb = jnp.pad(dt_bias, ((0, 0), (heads, LANE - 2 * heads))).reshape(depth, 1, LANE)
    hng = head_norm_g.reshape(depth, 1, dk)
    bdw, lng, lnb = b_dw.reshape(depth, 1, dc), ln_g.reshape(depth, 1, dc), ln_b.reshape(depth, 1, dc)

    hp, hs = x_prompt, x_sample
    sp_l, qp_l, gp_l = [], [], []
    s_s = q_s = g_s = None
    for l in range(depth):
        mod_p = mod[l, :bp].reshape(bp, 1, 3 * d)
        mod_s = mod[l, bp:].reshape(bs, 1, 3 * d)
        final = l == depth - 1

        proj, ba = _inproj(hp, mod_p, norm_g3, w_t, w_ba, l, bb=1, tt=tt_in, tn=tn,
                           n_a=4 * dd // tn, skip=2 * heads)
        o, s_new, qbuf = _delta_prompt(proj, ba, w_qkv_conv, alog, dtb, hng, l,
                                       b=bp, t=tp, nb=nb_delta, heads=heads, dk=dk)
        u, gbuf = _conf_prompt(proj, w_dw, bdw, lng, lnb, l, b=bp, t=tp, dd=dd, dc=dc, tt=tt_conf)
        hp = _outproj(hp, mod_p, o, u, w_out_b, final_g, l, bb=1, tt=tt_out, final=final)
        sp_l.append(s_new); qp_l.append(qbuf); gp_l.append(gbuf)

        proj, ba = _inproj(hs, mod_s, norm_g3, w_t, w_ba, l, bb=bb_s, tt=ts, tn=tn,
                           n_a=4 * dd // tn, skip=2 * heads)
        o, s_s, q_s = _delta_sample(proj, ba, qkv_hist, state_delta,
                                    None if l == 0 else (s_s, q_s), l, w_qkv_conv, alog, dtb, hng,
                                    b=bs, t=ts, nseq=nseq, heads=heads, dk=dk)
        u, g_s = _conf_sample(proj, glu_hist, g_s, l, w_dw, bdw, lng, lnb,
                              b=bs, t=ts, dd=dd, dc=dc, bb=bb_conf)
        hs = _outproj(hs, mod_s, o, u, w_out_b, final_g, l, bb=bb_out, tt=ts, final=final)

    return (hp, hs, jnp.stack(sp_l), jnp.stack(qp_l), jnp.stack(gp_l),
            s_s, jnp.swapaxes(q_s, 1, 2), jnp.swapaxes(g_s, 1, 2))
```

```python
import functools
import math

import jax
import jax.numpy as jnp
from jax import lax
from jax.experimental import pallas as pl
from jax.experimental.pallas import tpu as pltpu

F32 = jnp.float32
BF16 = jnp.bfloat16

EPS = 1e-6
QKV_CONV = 4
CONF_KERNEL = 31
CHUNK = 64
LANE = 128
SUBLANE = 8
QKV_HIST = SUBLANE
GLU_HIST = 32
CONF_PARTIALS = 3
VMEM_LIMIT_BYTES = 56 * 1024 * 1024


def _params(n_axes):
    return pltpu.CompilerParams(dimension_semantics=("arbitrary",) * n_axes,
                                vmem_limit_bytes=VMEM_LIMIT_BYTES)


def _sigmoid(x):
    return 0.5 * jnp.tanh(0.5 * x) + 0.5


def _glu(a, b):
    half = 0.5 * a
    return half * jnp.tanh(0.5 * b) + half


def _silu(x):
    half = 0.5 * x
    return half * jnp.tanh(half) + half


def _softplus(x):
    return jnp.maximum(x, 0.0) + jnp.log1p(jnp.exp(-jnp.abs(x)))


def _dot(a, b):
    return jnp.dot(a, b, preferred_element_type=F32)


def _dot_nt(a, b):
    return lax.dot_general(a, b, (((1,), (1,)), ((), ())), preferred_element_type=F32)


def _dot_tn(a, b):
    return lax.dot_general(a, b, (((0,), (0,)), ((), ())), preferred_element_type=F32)


def _ada_kernel(c_ref, w_ref, b_ref, o_ref):
    s = _silu(c_ref[...]).astype(BF16)
    o_ref[0] = _dot(s, w_ref[0].astype(BF16)) + b_ref[0]


def _ada(c_all, w_ada, b_ada):
    depth, d, n = w_ada.shape
    rows = c_all.shape[0]
    tn = n // 4 if n % (4 * LANE) == 0 else n
    return pl.pallas_call(
        _ada_kernel,
        out_shape=jax.ShapeDtypeStruct((depth, rows, n), F32),
        grid=(depth, n // tn),
        in_specs=[pl.BlockSpec((rows, d), lambda l, j: (0, 0)),
                  pl.BlockSpec((1, d, tn), lambda l, j: (l, 0, j)),
                  pl.BlockSpec((1, 1, tn), lambda l, j: (l, 0, j))],
        out_specs=pl.BlockSpec((1, rows, tn), lambda l, j: (l, 0, j)),
        compiler_params=_params(2),
        name="ada_mod",
    )(c_all, w_ada, b_ada.reshape(depth, 1, n))


def _inproj_kernel(x_ref, shift_ref, scale_ref, g_ref, w_ref, wba_ref, proj_ref, ba_ref, h_ref):
    @pl.when(pl.program_id(2) == 0)
    def _():
        x = x_ref[...]
        ms = jnp.mean(x * x, axis=-1, keepdims=True)
        gain = g_ref[0] * (1.0 + scale_ref[...])
        h = x * lax.rsqrt(ms + EPS) * gain + shift_ref[...]
        h = h.reshape(h_ref.shape).astype(BF16)
        h_ref[...] = h
        ba_ref[...] = _dot_nt(h, wba_ref[0].astype(BF16))

    proj_ref[...] = _dot_nt(h_ref[...], w_ref[0].astype(BF16))


def _inproj(x, mod, norm_g, w_t, w_ba, layer, *, bb, tt, tn, n_a, skip):
    b, t, d = x.shape
    n = w_t.shape[1] - skip
    nb, nt, nn = b // bb, t // tt, n // tn
    tm = bb * tt
    return pl.pallas_call(
        _inproj_kernel,
        out_shape=(jax.ShapeDtypeStruct((b * t, n), F32),
                   jax.ShapeDtypeStruct((b * t, LANE), F32)),
        grid=(nb, nt, nn),
        in_specs=[pl.BlockSpec((bb, tt, d), lambda i, j, k: (i, j, 0)),
                  pl.BlockSpec((bb, 1, d), lambda i, j, k: (i, 0, 0)),
                  pl.BlockSpec((bb, 1, d), lambda i, j, k: (i, 0, 1)),
                  pl.BlockSpec((1, 1, d), lambda i, j, k: (layer, 0, 0)),
                  pl.BlockSpec((pl.Element(1), pl.Element(tn), pl.Element(d)),
                               lambda i, j, k: (layer, pl.multiple_of(
                                   k * tn + jnp.where(k >= n_a, skip, 0), math.gcd(tn, skip)), 0)),
                  pl.BlockSpec((1, LANE, d), lambda i, j, k: (layer, 0, 0))],
        out_specs=(pl.BlockSpec((tm, tn), lambda i, j, k: (i * nt + j, k)),
                   pl.BlockSpec((tm, LANE), lambda i, j, k: (i * nt + j, 0))),
        scratch_shapes=[pltpu.VMEM((tm, d), BF16)],
        compiler_params=_params(3),
        name="in_proj",
    )(x, mod, mod, norm_g, w_t, w_ba)


def _short_conv(tiles, pick_cur, pick_prev, wconv_ref):
    cur, prev = pick_cur(tiles), pick_prev(tiles)
    sub = lax.broadcasted_iota(jnp.int32, cur.shape, cur.ndim - 2)
    acc = wconv_ref[0, QKV_CONV - 1:QKV_CONV, :] * cur
    for d in range(1, QKV_CONV):
        delayed = pltpu.roll(jnp.where(sub < SUBLANE - d, cur, prev), d, axis=1)
        acc = acc + wconv_ref[0, QKV_CONV - 1 - d:QKV_CONV - d, :] * delayed
    return acc


def _gates(ba, alog, dtb):
    beta = _sigmoid(ba)
    g = -jnp.exp(alog) * _softplus(ba + dtb)
    return beta, g


def _delta_rows(qkv, z_ref, ba, alog, dtb, hng, s_prev, *, c, nseq, heads, dk):
    rows_n = nseq * c
    dd = heads * dk
    hs = range(heads)
    rows = lax.broadcasted_iota(jnp.int32, (rows_n, rows_n), 0)
    cols = lax.broadcasted_iota(jnp.int32, (rows_n, rows_n), 1)
    if nseq > 1:
        shift = c.bit_length() - 1
        same = (rows >> shift) == (cols >> shift)
        causal = same & (rows >= cols)
        strict = same & (rows > cols)
        upper = same & (rows <= cols)
        ones = jnp.where(same, 1.0, 0.0).astype(BF16)
    else:
        causal = rows >= cols
        strict = rows > cols
        upper = rows <= cols
        ones = jnp.ones((rows_n, rows_n), BF16)
    lower_b = jnp.where(causal, 1.0, 0.0).astype(BF16)
    upper_b = jnp.where(upper, 1.0, 0.0).astype(BF16)
    eye = jnp.where(rows == cols, 1.0, 0.0)

    beta, g = _gates(ba, alog, dtb)
    g0 = g.astype(BF16)
    r1 = g - g0.astype(F32)
    g1 = r1.astype(BF16)
    g2 = (r1 - g1.astype(F32)).astype(BF16)
    gc = _dot(lower_b, g0) + (_dot(lower_b, g1) + _dot(lower_b, g2))
    gct = _dot_tn(g0, upper_b) + (_dot_tn(g1, upper_b) + _dot_tn(g2, upper_b))
    gl = _dot(ones, g0) + (_dot(ones, g1) + _dot(ones, g2))
    yield

    def l2n(x, scale):
        return x * (lax.rsqrt(jnp.sum(x * x, axis=-1, keepdims=True) + EPS) * scale)

    qn = [l2n(qkv[:, h * dk:(h + 1) * dk], dk ** -0.5) for h in hs]
    kn = [l2n(qkv[:, dd + h * dk:dd + (h + 1) * dk], 1.0) for h in hs]
    vv = [qkv[:, 2 * dd + h * dk:2 * dd + (h + 1) * dk] for h in hs]
    bcol = [beta[:, h:h + 1] for h in hs]
    gcol = [gc[:, heads + h:heads + h + 1] for h in hs]
    glc = [gl[:, heads + h:heads + h + 1] for h in hs]
    decay = [jnp.where(causal, jnp.exp(jnp.where(causal, gcol[h] - gct[heads + h:heads + h + 1, :], 0.0)), 0.0)
             for h in hs]
    kb = [kn[h].astype(BF16) for h in hs]
    kq = [jnp.concatenate([kn[h], qn[h]], axis=0).astype(BF16) for h in hs]
    kkqk = [_dot_nt(kq[h], kb[h]) for h in hs]
    yield
    low = [jnp.where(strict, bcol[h] * kkqk[h][:rows_n] * decay[h], 0.0) for h in hs]
    attn = [(kkqk[h][rows_n:] * decay[h]).astype(BF16) for h in hs]
    inv = [eye - jnp.where((rows ^ cols) == 1, low[h], 0.0) for h in hs]
    b = 2
    while b < c:
        pair = ((rows ^ cols) >= b) & ((rows ^ cols) < 2 * b)
        off = [jnp.where(pair, low[h], 0.0).astype(BF16) for h in hs]
        invb = [inv[h].astype(BF16) for h in hs]
        y = [_dot(off[h], invb[h]).astype(BF16) for h in hs]
        yield
        inv = [inv[h] - _dot(invb[h], y[h]) for h in hs]
        yield
        b *= 2
    egc = [jnp.exp(gcol[h]) for h in hs]
    rhs = [jnp.concatenate([bcol[h] * vv[h], (bcol[h] * egc[h]) * kn[h]], axis=1).astype(BF16) for h in hs]
    sol = [_dot(inv[h].astype(BF16), rhs[h]) for h in hs]
    yield
    qg = [qn[h] * egc[h] for h in hs]
    kdb = [(kn[h] * jnp.exp(glc[h] - gcol[h])).astype(BF16) for h in hs]

    if nseq == 1:
        sb = [s_prev[0][h].astype(BF16) for h in hs]
        wq = [jnp.concatenate([sol[h][:, dk:], qg[h]], axis=0).astype(BF16) for h in hs]
        ys = [_dot(wq[h], sb[h]) for h in hs]
        yield
        ub = [(sol[h][:, :dk] - ys[h][:rows_n]).astype(BF16) for h in hs]
        o = [ys[h][rows_n:] + _dot(attn[h], ub[h]) for h in hs]
        yield
        s_new = [[jnp.exp(glc[h][0:1, :]) * s_prev[0][h] + _dot_tn(kdb[h], ub[h]) for h in hs]]
    else:
        js = range(nseq)
        sb = [[s_prev[j][h].astype(BF16) for h in hs] for j in js]
        wq = [[jnp.concatenate([sol[h][j * c:(j + 1) * c, dk:], qg[h][j * c:(j + 1) * c]], axis=0).astype(BF16)
               for h in hs] for j in js]
        ys = [[_dot(wq[j][h], sb[j][h]) for h in hs] for j in js]
        ws = [jnp.concatenate([ys[j][h][:c] for j in js], axis=0) for h in hs]
        qs = [jnp.concatenate([ys[j][h][c:] for j in js], axis=0) for h in hs]
        u = [sol[h][:, :dk] - ws[h] for h in hs]
        o = [qs[h] + _dot(attn[h], u[h].astype(BF16)) for h in hs]
        seq_of_row = lax.broadcasted_iota(jnp.int32, (rows_n, dk), 0) >> (c.bit_length() - 1)
        s_new = [[None] * heads for _ in js]
        for j in range(0, nseq, 2):
            um = [jnp.concatenate([jnp.where(seq_of_row == j, u[h], 0.0),
                                   jnp.where(seq_of_row == j + 1, u[h], 0.0)], axis=1).astype(BF16) for h in hs]
            sn = [_dot_tn(kdb[h], um[h]) for h in hs]
            for h in hs:
                s_new[j][h] = jnp.exp(glc[h][j * c:j * c + 1, :]) * s_prev[j][h] + sn[h][:, :dk]
                s_new[j + 1][h] = (jnp.exp(glc[h][(j + 1) * c:(j + 1) * c + 1, :]) * s_prev[j + 1][h]
                                   + sn[h][:, dk:])
    out = []
    for h in hs:
        on = o[h] * lax.rsqrt(jnp.mean(o[h] * o[h], axis=-1, keepdims=True) + EPS) * hng
        out.append(on * _silu(z_ref[:, h * dk:(h + 1) * dk]))
    return out, s_new


def _lockstep(gens):
    results = [None] * len(gens)
    live = list(range(len(gens)))
    while live:
        for idx in list(live):
            try:
                next(gens[idx])
            except StopIteration as stop:
                results[idx] = stop.value
                live.remove(idx)
    return results


def _delta_prompt_kernel(qkv_ref, z_ref, ba_ref, wconv_ref, alog_ref, dtb_ref, hng_ref,
                         o_ref, s_ref, qbuf_ref, ext_ref, *, c, nb, heads, dk):
    i = pl.program_id(1)

    @pl.when(i == 0)
    def _():
        ext_ref[...] = jnp.zeros(ext_ref.shape, F32)
        s_ref[...] = jnp.zeros(s_ref.shape, F32)

    gens = []
    for g in range(nb):
        xe = jnp.concatenate([ext_ref[g], qkv_ref[g]], axis=0)
        tiles = xe.reshape((QKV_HIST + c) // SUBLANE, SUBLANE, xe.shape[1])
        acc = _short_conv(tiles, lambda a: a[1:], lambda a: a[:-1], wconv_ref).reshape(c, xe.shape[1])
        ext_ref[g] = qkv_ref[g, c - QKV_HIST:c, :]
        s_prev = [[s_ref[g, h] for h in range(heads)]]
        gens.append(_delta_rows(_silu(acc), z_ref.at[g], ba_ref[g], alog_ref[0], dtb_ref[0], hng_ref[0],
                                s_prev, c=c, nseq=1, heads=heads, dk=dk))
    for g, (out, s_new) in enumerate(_lockstep(gens)):
        for h in range(heads):
            o_ref[g, :, h * dk:(h + 1) * dk] = out[h].astype(o_ref.dtype)
            s_ref[g, h] = s_new[0][h]

    @pl.when(i == pl.num_programs(1) - 1)
    def _():
        qbuf_ref[...] = qkv_ref[:, c - (QKV_CONV - 1):c, :]


def _layer_spec(arr, layer):
    zeros = (0,) * (arr.ndim - 1)
    return pl.BlockSpec((1,) + arr.shape[1:], lambda *_: (layer,) + zeros)


def _delta_prompt(proj, ba, wconv, alog, dtb, hng, layer, *, b, t, nb, heads, dk):
    dd = heads * dk
    c = min(CHUNK, t)
    nc = t // c
    proj3 = proj.reshape(b, t, proj.shape[1])
    kern = functools.partial(_delta_prompt_kernel, c=c, nb=nb, heads=heads, dk=dk)
    o, s_new, qbuf = pl.pallas_call(
        kern,
        out_shape=(jax.ShapeDtypeStruct((b, t, dd), BF16),
                   jax.ShapeDtypeStruct((b, heads, dk, dk), F32),
                   jax.ShapeDtypeStruct((b, QKV_CONV - 1, 3 * dd), F32)),
        grid=(b // nb, nc),
        in_specs=[pl.BlockSpec((nb, c, 3 * dd), lambda i, j: (i, j, 0)),
                  pl.BlockSpec((nb, c, dd), lambda i, j: (i, j, 3)),
                  pl.BlockSpec((nb, c, LANE), lambda i, j: (i, j, 0)),
                  _layer_spec(wconv, layer), _layer_spec(alog, layer), _layer_spec(dtb, layer),
                  _layer_spec(hng, layer)],
        out_specs=(pl.BlockSpec((nb, c, dd), lambda i, j: (i, j, 0)),
                   pl.BlockSpec((nb, heads, dk, dk), lambda i, j: (i, 0, 0, 0)),
                   pl.BlockSpec((nb, QKV_CONV - 1, 3 * dd), lambda i, j: (i, 0, 0))),
        scratch_shapes=[pltpu.VMEM((nb, QKV_HIST, 3 * dd), F32)],
        compiler_params=_params(2),
        name="delta_prompt",
    )(proj3, proj3, ba.reshape(b, t, LANE), wconv, alog, dtb, hng)
    return o.reshape(b * t, dd), s_new, qbuf


def _write_slab(ref, slab, value):
    for other in range(ref.shape[0]):
        if other != slab:
            ref[other] = jnp.zeros(ref.shape[1:], ref.dtype)
    ref[slab] = value


def _delta_sample_kernel(qkv_ref, z_ref, ba_ref, hist_ref, s0_ref, wconv_ref, alog_ref, dtb_ref,
                         hng_ref, *rest, t, nseq, heads, dk, slab):
    o_ref, s_ref, qbuf_ref, ext_ref = rest[-4:]
    n = ext_ref.shape[2]
    first = QKV_HIST - (QKV_CONV - 1)
    ext_ref[:, 0:first, :] = jnp.zeros((nseq, first, n), F32)
    for r in range(QKV_CONV - 1):
        ext_ref[:, first + r, :] = hist_ref[0, r]
    ext_ref[:, QKV_HIST:QKV_HIST + t, :] = qkv_ref[...].reshape(nseq, t, n)
    per_seq = (QKV_HIST + t) // SUBLANE
    tiles = ext_ref[...].reshape(nseq * per_seq, SUBLANE, n)

    def pick(a, lo, hi):
        return a.reshape(nseq, per_seq, SUBLANE, n)[:, lo:hi].reshape(nseq * (per_seq - 1), SUBLANE, n)

    acc = _short_conv(tiles, lambda a: pick(a, 1, per_seq), lambda a: pick(a, 0, per_seq - 1),
                      wconv_ref).reshape(nseq * t, n)
    _write_slab(qbuf_ref, slab, jnp.stack(
        [ext_ref[:, QKV_HIST + t - (QKV_CONV - 1) + r, :] for r in range(QKV_CONV - 1)]))

    s_prev = [[s0_ref[0, j, h] for h in range(heads)] for j in range(nseq)]
    (out, s_new), = _lockstep([_delta_rows(_silu(acc), z_ref, ba_ref[...], alog_ref[0], dtb_ref[0],
                                           hng_ref[0], s_prev, c=t, nseq=nseq, heads=heads, dk=dk)])
    for other in range(s_ref.shape[0]):
        if other != slab:
            s_ref[other] = jnp.zeros(s_ref.shape[1:], F32)
    for h in range(heads):
        o_ref[:, h * dk:(h + 1) * dk] = out[h].astype(o_ref.dtype)
        for j in range(nseq):
            s_ref[slab, j, h] = s_new[j][h]


def _layered_out(depth, layer, prev, block_tail, n_in, batch_pos=0):
    def index(first, i):
        tail = [0] * len(block_tail)
        tail[batch_pos] = i
        return (first,) + tuple(tail)

    if prev is None:
        return pl.BlockSpec((depth,) + block_tail, lambda i: index(0, i)), layer, None
    return pl.BlockSpec((1,) + block_tail, lambda i: index(layer, i)), 0, n_in


def _delta_sample(proj, ba, hist, state, prev, layer, wconv, alog, dtb, hng, *, b, t, nseq, heads, dk):
    dd = heads * dk
    depth = state.shape[0]
    rows_n = nseq * t
    in_specs = [pl.BlockSpec((rows_n, 3 * dd), lambda i: (i, 0)),
                pl.BlockSpec((rows_n, dd), lambda i: (i, 3)),
                pl.BlockSpec((rows_n, LANE), lambda i: (i, 0)),
                pl.BlockSpec((1, QKV_CONV - 1, nseq, 3 * dd), lambda i: (layer, 0, i, 0)),
                pl.BlockSpec((1, nseq, heads, dk, dk), lambda i: (layer, i, 0, 0, 0)),
                _layer_spec(wconv, layer), _layer_spec(alog, layer), _layer_spec(dtb, layer),
                _layer_spec(hng, layer)]
    args = [proj, proj, ba, hist, state, wconv, alog, dtb, hng]
    s_spec, slab, alias_at = _layered_out(depth, layer, prev, (nseq, heads, dk, dk), len(args))
    q_spec, _, _ = _layered_out(depth, layer, prev, (QKV_CONV - 1, nseq, 3 * dd), len(args), batch_pos=1)
    aliases = {}
    if prev is not None:
        in_specs += [pl.BlockSpec(memory_space=pl.ANY)] * 2
        args += list(prev)
        aliases = {alias_at: 1, alias_at + 1: 2}
    kern = functools.partial(_delta_sample_kernel, t=t, nseq=nseq, heads=heads, dk=dk, slab=slab)
    return pl.pallas_call(
        kern,
        out_shape=(jax.ShapeDtypeStruct((b * t, dd), BF16),
                   jax.ShapeDtypeStruct((depth, b, heads, dk, dk), F32),
                   jax.ShapeDtypeStruct((depth, QKV_CONV - 1, b, 3 * dd), F32)),
        grid=(b // nseq,),
        in_specs=in_specs,
        out_specs=(pl.BlockSpec((rows_n, dd), lambda i: (i, 0)), s_spec, q_spec),
        scratch_shapes=[pltpu.VMEM((nseq, QKV_HIST + t, 3 * dd), F32)],
        input_output_aliases=aliases,
        compiler_params=_params(1),
        name="delta_sample",
    )(*args)


def _conf_epilogue(conv, zc, bdw, lng, lnb):
    y = conv + bdw
    mu = jnp.mean(y, axis=-1, keepdims=True)
    yc = y - mu
    var = jnp.mean(yc * yc, axis=-1, keepdims=True)
    y = yc * lax.rsqrt(var + EPS) * lng + lnb
    return _silu(y) * _silu(zc)


def _conf_tile(ga_ref, gb_ref, zc_ref, w_ref, bdw_ref, lng_ref, lnb_ref, ext_ref, sh_ref, conv_ref,
               *, tt, rr):
    dc = ext_ref.shape[1]
    first = GLU_HIST - (CONF_KERNEL - 1)
    ext_ref[GLU_HIST:GLU_HIST + tt, :] = _glu(ga_ref[...], gb_ref[...])
    for p in range(SUBLANE):
        for lt in range(dc // LANE):
            sh_ref[p, lt] = ext_ref[p:p + GLU_HIST + tt, lt * LANE:(lt + 1) * LANE]
    for lt in range(dc // LANE):
        lanes = slice(lt * LANE, (lt + 1) * LANE)
        taps = [jnp.broadcast_to(w_ref[0, j:j + 1, lanes], (rr, LANE)) for j in range(CONF_KERNEL)]

        def body(r, carry, lt=lt, lanes=lanes, taps=taps):
            r0 = pl.multiple_of(r * rr, rr)
            parts = [None] * CONF_PARTIALS
            for j in range(CONF_KERNEL):
                off = first + j
                win = sh_ref[off % SUBLANE, lt, pl.ds(r0 + (off // SUBLANE) * SUBLANE, rr), :]
                p = j % CONF_PARTIALS
                parts[p] = taps[j] * win if parts[p] is None else parts[p] + taps[j] * win
            acc = parts[0]
            for p in range(1, CONF_PARTIALS):
                acc = acc + parts[p]
            conv_ref[pl.ds(r0, rr), lanes] = acc
            return carry

        lax.fori_loop(0, tt // rr, body, 0)
    y = _conf_epilogue(conv_ref[...], zc_ref[...], bdw_ref[0], lng_ref[0], lnb_ref[0])
    ext_ref[0:GLU_HIST, :] = ext_ref[tt:tt + GLU_HIST, :]
    return y


def _conf_prompt_kernel(ga_ref, gb_ref, zc_ref, w_ref, bdw_ref, lng_ref, lnb_ref,
                        u_ref, gbuf_ref, ext_ref, sh_ref, conv_ref, *, tt, rr):
    i = pl.program_id(1)
    dc = ext_ref.shape[1]

    @pl.when(i == 0)
    def _():
        ext_ref[0:GLU_HIST, :] = jnp.zeros((GLU_HIST, dc), F32)
        ext_ref[GLU_HIST + tt:GLU_HIST + tt + SUBLANE, :] = jnp.zeros((SUBLANE, dc), F32)

    y = _conf_tile(ga_ref, gb_ref, zc_ref, w_ref, bdw_ref, lng_ref, lnb_ref, ext_ref, sh_ref, conv_ref,
                   tt=tt, rr=rr)
    u_ref[...] = y.astype(u_ref.dtype)

    @pl.when(i == pl.num_programs(1) - 1)
    def _():
        gbuf_ref[0] = ext_ref[GLU_HIST - (CONF_KERNEL - 1):GLU_HIST, :]


def _conf_prompt(proj, w_dw, b_dw, ln_g, ln_b, layer, *, b, t, dd, dc, tt):
    nt = t // tt
    rr = min(32, tt)
    col0 = (4 * dd) // dc
    kern = functools.partial(_conf_prompt_kernel, tt=tt, rr=rr)
    row = lambda i, j: i * nt + j
    return pl.pallas_call(
        kern,
        out_shape=(jax.ShapeDtypeStruct((b * t, dc), BF16),
                   jax.ShapeDtypeStruct((b, CONF_KERNEL - 1, dc), F32)),
        grid=(b, nt),
        in_specs=[pl.BlockSpec((tt, dc), lambda i, j: (row(i, j), col0)),
                  pl.BlockSpec((tt, dc), lambda i, j: (row(i, j), col0 + 1)),
                  pl.BlockSpec((tt, dc), lambda i, j: (row(i, j), col0 + 2)),
                  _layer_spec(w_dw, layer), _layer_spec(b_dw, layer), _layer_spec(ln_g, layer),
                  _layer_spec(ln_b, layer)],
        out_specs=(pl.BlockSpec((tt, dc), lambda i, j: (row(i, j), 0)),
                   pl.BlockSpec((1, CONF_KERNEL - 1, dc), lambda i, j: (i, 0, 0))),
        scratch_shapes=[pltpu.VMEM((GLU_HIST + tt + SUBLANE, dc), F32),
                        pltpu.VMEM((SUBLANE, dc // LANE, GLU_HIST + tt, LANE), F32),
                        pltpu.VMEM((tt, dc), F32)],
        compiler_params=_params(2),
        name="conf_prompt",
    )(proj, proj, proj, w_dw, b_dw, ln_g, ln_b)


def _conf_sample_kernel(ga_ref, gb_ref, zc_ref, hist_ref, w_ref, bdw_ref, lng_ref, lnb_ref,
                        *rest, t, bb, slab):
    u_ref, gbuf_ref, ext_ref, conv_ref, glu_ref = rest[-5:]
    dc = conv_ref.shape[2]
    hist = CONF_KERNEL - 1
    for other in range(gbuf_ref.shape[0]):
        if other != slab:
            gbuf_ref[other] = jnp.zeros(gbuf_ref.shape[1:], F32)
    glu_ref[...] = _glu(ga_ref[...], gb_ref[...])
    for lt in range(dc // LANE):
        lanes = slice(lt * LANE, (lt + 1) * LANE)
        ext_ref[lt, 0:hist] = hist_ref[0, :, :, lanes]
        for step in range(t):
            ext_ref[lt, hist + step] = glu_ref[:, step, lanes]
        acc = None
        for j in range(CONF_KERNEL):
            term = w_ref[0, j:j + 1, lanes] * ext_ref[lt, j:j + t]
            acc = term if acc is None else acc + term
        conv_ref[:, :, lanes] = acc
        gbuf_ref[slab, :, :, lanes] = ext_ref[lt, t:t + hist]
    zc = jnp.stack([zc_ref[:, step, :] for step in range(t)])
    y = _conf_epilogue(conv_ref[...], zc, bdw_ref[0], lng_ref[0], lnb_ref[0])
    for step in range(t):
        u_ref[:, step, :] = y[step]


def _conf_sample(proj, hist, prev, layer, w_dw, b_dw, ln_g, ln_b, *, b, t, dd, dc, bb):
    depth = hist.shape[0]
    col0 = (4 * dd) // dc
    proj = proj.reshape(b, t, proj.shape[1])
    in_specs = [pl.BlockSpec((bb, t, dc), lambda i: (i, 0, col0)),
                pl.BlockSpec((bb, t, dc), lambda i: (i, 0, col0 + 1)),
                pl.BlockSpec((bb, t, dc), lambda i: (i, 0, col0 + 2)),
                pl.BlockSpec((1, CONF_KERNEL - 1, bb, dc), lambda i: (layer, 0, i, 0)),
                _layer_spec(w_dw, layer), _layer_spec(b_dw, layer), _layer_spec(ln_g, layer),
                _layer_spec(ln_b, layer)]
    args = [proj, proj, proj, hist, w_dw, b_dw, ln_g, ln_b]
    g_spec, slab, alias_at = _layered_out(depth, layer, prev, (CONF_KERNEL - 1, bb, dc), len(args),
                                          batch_pos=1)
    aliases = {}
    if prev is not None:
        in_specs.append(pl.BlockSpec(memory_space=pl.ANY))
        args.append(prev)
        aliases = {alias_at: 1}
    kern = functools.partial(_conf_sample_kernel, t=t, bb=bb, slab=slab)
    u, gbuf = pl.pallas_call(
        kern,
        out_shape=(jax.ShapeDtypeStruct((b, t, dc), F32),
                   jax.ShapeDtypeStruct((depth, CONF_KERNEL - 1, b, dc), F32)),
        grid=(b // bb,),
        in_specs=in_specs,
        out_specs=(pl.BlockSpec((bb, t, dc), lambda i: (i, 0, 0)), g_spec),
        scratch_shapes=[pltpu.VMEM((dc // LANE, CONF_KERNEL - 1 + t, bb, LANE), F32),
                        pltpu.VMEM((t, bb, dc), F32),
                        pltpu.VMEM((bb, t, dc), F32)],
        input_output_aliases=aliases,
        compiler_params=_params(1),
        name="conf_sample",
    )(*args)
    return u.reshape(b * t, dc), gbuf


def _outproj_kernel(x_ref, gate_ref, o_ref, u_ref, wo_ref, wu_ref, fg_ref, y_ref, *, final):
    mix = _dot(o_ref[...], wo_ref[0]) + _dot(u_ref[...].astype(BF16), wu_ref[0])
    x = x_ref[...]
    y = x + gate_ref[...] * mix.reshape(x.shape)
    if final:
        y = y * lax.rsqrt(jnp.mean(y * y, axis=-1, keepdims=True) + EPS) * fg_ref[...]
    y_ref[...] = y


def _outproj(x, mod, o, u, w_out, final_g, layer, *, bb, tt, final):
    b, t, d = x.shape
    dd, dc = o.shape[1], u.shape[1]
    nt = t // tt
    tm = bb * tt
    kern = functools.partial(_outproj_kernel, final=final)
    return pl.pallas_call(
        kern,
        out_shape=jax.ShapeDtypeStruct((b, t, d), F32),
        grid=(b // bb, nt),
        in_specs=[pl.BlockSpec((bb, tt, d), lambda i, j: (i, j, 0)),
                  pl.BlockSpec((bb, 1, d), lambda i, j: (i, 0, 2)),
                  pl.BlockSpec((tm, dd), lambda i, j: (i * nt + j, 0)),
                  pl.BlockSpec((tm, dc), lambda i, j: (i * nt + j, 0)),
                  pl.BlockSpec((1, dd, d), lambda i, j: (layer, 0, 0)),
                  pl.BlockSpec((1, dc, d), lambda i, j: (layer, dd // dc, 0)),
                  pl.BlockSpec((1, d), lambda i, j: (0, 0))],
        out_specs=pl.BlockSpec((bb, tt, d), lambda i, j: (i, j, 0)),
        compiler_params=_params(2),
        name="out_proj",
    )(x, mod, o, u, w_out, w_out, final_g.reshape(1, d))


def kernel(x_prompt, x_sample, c_prompt, c_sample, state_delta, state_qkv_conv, state_glu_conv,
           norm_g, w_ada, b_ada, w_in, w_qkv_conv, a_log, dt_bias, head_norm_g, w_dw, b_dw,
           ln_g, ln_b, w_out, final_g):
    depth = w_in.shape[0]
    bp, tp, d = x_prompt.shape
    bs, ts, _ = x_sample.shape
    heads = a_log.shape[1]
    dd = w_qkv_conv.shape[2] // 3
    dk = dd // heads
    dc = w_dw.shape[2]
    assert ts % SUBLANE == 0 and ts >= QKV_CONV - 1 and ts <= CONF_KERNEL - 1
    assert 2 * heads <= LANE and dd % dc == 0 and tp % CHUNK == 0
    assert CHUNK % ts == 0 and bs % (CHUNK // ts) == 0 and (CHUNK // ts) % 2 == 0

    mod = _ada(jnp.concatenate([c_prompt, c_sample], axis=0), w_ada, b_ada)

    tn = min(1024, dd, dc)
    tm = 1024
    tt_in = min(tm, tp)
    bb_s = min(bs, tm // ts)
    tt_out = min(512, tp)
    bb_out = min(bs, 512 // ts)
    tt_conf = min(256, tp)
    bb_conf = min(bs, 16)
    nb_delta = 4 if bp % 4 == 0 else 1
    nseq = CHUNK // ts

    w_t = jnp.swapaxes(w_in, 1, 2)
    w_ba = jnp.pad(w_t[:, 4 * dd:4 * dd + 2 * heads], ((0, 0), (0, LANE - 2 * heads), (0, 0)))
    w_out_b = w_out.astype(BF16)
    qkv_hist = jnp.swapaxes(state_qkv_conv, 1, 2)
    glu_hist = jnp.swapaxes(state_glu_conv, 1, 2)
    norm_g3 = norm_g.reshape(depth, 1, d)
    alog = jnp.pad(a_log, ((0, 0), (heads, LANE - 2 * heads))).reshape(depth, 1, LANE)
    dtb = jnp.pad(dt_bias, ((0, 0), (heads, LANE - 2 * heads))).reshape(depth, 1, LANE)
    hng = head_norm_g.reshape(depth, 1, dk)
    bdw, lng, lnb = b_dw.reshape(depth, 1, dc), ln_g.reshape(depth, 1, dc), ln_b.reshape(depth, 1, dc)

    hp, hs = x_prompt, x_sample
    sp_l, qp_l, gp_l = [], [], []
    s_s = q_s = g_s = None
    for l in range(depth):
        mod_p = mod[l, :bp].reshape(bp, 1, 3 * d)
        mod_s = mod[l, bp:].reshape(bs, 1, 3 * d)
        final = l == depth - 1

        proj, ba = _inproj(hp, mod_p, norm_g3, w_t, w_ba, l, bb=1, tt=tt_in, tn=tn,
                           n_a=4 * dd // tn, skip=2 * heads)
        o, s_new, qbuf = _delta_prompt(proj, ba, w_qkv_conv, alog, dtb, hng, l,
                                       b=bp, t=tp, nb=nb_delta, heads=heads, dk=dk)
        u, gbuf = _conf_prompt(proj, w_dw, bdw, lng, lnb, l, b=bp, t=tp, dd=dd, dc=dc, tt=tt_conf)
        hp = _outproj(hp, mod_p, o, u, w_out_b, final_g, l, bb=1, tt=tt_out, final=final)
        sp_l.append(s_new); qp_l.append(qbuf); gp_l.append(gbuf)

        proj, ba = _inproj(hs, mod_s, norm_g3, w_t, w_ba, l, bb=bb_s, tt=ts, tn=tn,
                           n_a=4 * dd // tn, skip=2 * heads)
        o, s_s, q_s = _delta_sample(proj, ba, qkv_hist, state_delta,
                                    None if l == 0 else (s_s, q_s), l, w_qkv_conv, alog, dtb, hng,
                                    b=bs, t=ts, nseq=nseq, heads=heads, dk=dk)
        u, g_s = _conf_sample(proj, glu_hist, g_s, l, w_dw, bdw, lng, lnb,
                              b=bs, t=ts, dd=dd, dc=dc, bb=bb_conf)
        hs = _outproj(hs, mod_s, o, u, w_out_b, final_g, l, bb=bb_out, tt=ts, final=final)

    return (hp, hs, jnp.stack(sp_l), jnp.stack(qp_l), jnp.stack(gp_l),
            s_s, jnp.swapaxes(q_s, 1, 2), jnp.swapaxes(g_s, 1, 2))
```

```python
import functools
import math

import jax
import jax.numpy as jnp
from jax import lax
from jax.experimental import pallas as pl
from jax.experimental.pallas import tpu as pltpu

F32 = jnp.float32
BF16 = jnp.bfloat16

EPS = 1e-6
QKV_CONV = 4
CONF_KERNEL = 31
CHUNK = 64
LANE = 128
SUBLANE = 8
QKV_HIST = SUBLANE
GLU_HIST = 32
CONF_PARTIALS = 3
VMEM_LIMIT_BYTES = 56 * 1024 * 1024


def _params(n_axes):
    return pltpu.CompilerParams(dimension_semantics=("arbitrary",) * n_axes,
                                vmem_limit_bytes=VMEM_LIMIT_BYTES)


def _sigmoid(x):
    return 0.5 * jnp.tanh(0.5 * x) + 0.5


def _glu(a, b):
    half = 0.5 * a
    return half * jnp.tanh(0.5 * b) + half


def _silu(x):
    half = 0.5 * x
    return half * jnp.tanh(half) + half


def _softplus(x):
    return jnp.maximum(x, 0.0) + jnp.log1p(jnp.exp(-jnp.abs(x)))


def _dot(a, b):
    return jnp.dot(a, b, preferred_element_type=F32)


def _dot_nt(a, b):
    return lax.dot_general(a, b, (((1,), (1,)), ((), ())), preferred_element_type=F32)


def _dot_tn(a, b):
    return lax.dot_general(a, b, (((0,), (0,)), ((), ())), preferred_element_type=F32)


def _ada_kernel(c_ref, w_ref, b_ref, o_ref):
    s = _silu(c_ref[...]).astype(BF16)
    o_ref[0] = _dot(s, w_ref[0].astype(BF16)) + b_ref[0]


def _ada(c_all, w_ada, b_ada):
    depth, d, n = w_ada.shape
    rows = c_all.shape[0]
    tn = n // 4 if n % (4 * LANE) == 0 else n
    return pl.pallas_call(
        _ada_kernel,
        out_shape=jax.ShapeDtypeStruct((depth, rows, n), F32),
        grid=(depth, n // tn),
        in_specs=[pl.BlockSpec((rows, d), lambda l, j: (0, 0)),
                  pl.BlockSpec((1, d, tn), lambda l, j: (l, 0, j)),
                  pl.BlockSpec((1, 1, tn), lambda l, j: (l, 0, j))],
        out_specs=pl.BlockSpec((1, rows, tn), lambda l, j: (l, 0, j)),
        compiler_params=_params(2),
        name="ada_mod",
    )(c_all, w_ada, b_ada.reshape(depth, 1, n))


def _inproj_kernel(x_ref, shift_ref, scale_ref, g_ref, w_ref, wba_ref, proj_ref, ba_ref, h_ref):
    @pl.when(pl.program_id(2) == 0)
    def _():
        x = x_ref[...]
        ms = jnp.mean(x * x, axis=-1, keepdims=True)
        gain = g_ref[0] * (1.0 + scale_ref[...])
        h = x * lax.rsqrt(ms + EPS) * gain + shift_ref[...]
        h = h.reshape(h_ref.shape).astype(BF16)
        h_ref[...] = h
        ba_ref[...] = _dot_nt(h, wba_ref[0].astype(BF16))

    proj_ref[...] = _dot_nt(h_ref[...], w_ref[0].astype(BF16))


def _inproj(x, mod, norm_g, w_t, w_ba, layer, *, bb, tt, tn, n_a, skip):
    b, t, d = x.shape
    n = w_t.shape[1] - skip
    nb, nt, nn = b // bb, t // tt, n // tn
    tm = bb * tt
    return pl.pallas_call(
        _inproj_kernel,
        out_shape=(jax.ShapeDtypeStruct((b * t, n), F32),
                   jax.ShapeDtypeStruct((b * t, LANE), F32)),
        grid=(nb, nt, nn),
        in_specs=[pl.BlockSpec((bb, tt, d), lambda i, j, k: (i, j, 0)),
                  pl.BlockSpec((bb, 1, d), lambda i, j, k: (i, 0, 0)),
                  pl.BlockSpec((bb, 1, d), lambda i, j, k: (i, 0, 1)),
                  pl.BlockSpec((1, 1, d), lambda i, j, k: (layer, 0, 0)),
                  pl.BlockSpec((pl.Element(1), pl.Element(tn), pl.Element(d)),
                               lambda i, j, k: (layer, pl.multiple_of(
                                   k * tn + jnp.where(k >= n_a, skip, 0), math.gcd(tn, skip)), 0)),
                  pl.BlockSpec((1, LANE, d), lambda i, j, k: (layer, 0, 0))],
        out_specs=(pl.BlockSpec((tm, tn), lambda i, j, k: (i * nt + j, k)),
                   pl.BlockSpec((tm, LANE), lambda i, j, k: (i * nt + j, 0))),
        scratch_shapes=[pltpu.VMEM((tm, d), BF16)],
        compiler_params=_params(3),
        name="in_proj",
    )(x, mod, mod, norm_g, w_t, w_ba)


def _short_conv(tiles, pick_cur, pick_prev, wconv_ref):
    cur, prev = pick_cur(tiles), pick_prev(tiles)
    sub = lax.broadcasted_iota(jnp.int32, cur.shape, cur.ndim - 2)
    acc = wconv_ref[0, QKV_CONV - 1:QKV_CONV, :] * cur
    for d in range(1, QKV_CONV):
        delayed = pltpu.roll(jnp.where(sub < SUBLANE - d, cur, prev), d, axis=1)
        acc = acc + wconv_ref[0, QKV_CONV - 1 - d:QKV_CONV - d, :] * delayed
    return acc


def _gates(ba, alog, dtb):
    beta = _sigmoid(ba)
    g = -jnp.exp(alog) * _softplus(ba + dtb)
    return beta, g


def _delta_rows(qkv, z_ref, ba, alog, dtb, hng, s_prev, *, c, nseq, heads, dk):
    rows_n = nseq * c
    dd = heads * dk
    hs = range(heads)
    rows = lax.broadcasted_iota(jnp.int32, (rows_n, rows_n), 0)
    cols = lax.broadcasted_iota(jnp.int32, (rows_n, rows_n), 1)
    if nseq > 1:
        shift = c.bit_length() - 1
        same = (rows >> shift) == (cols >> shift)
        causal = same & (rows >= cols)
        strict = same & (rows > cols)
        upper = same & (rows <= cols)
        ones = jnp.where(same, 1.0, 0.0).astype(BF16)
    else:
        causal = rows >= cols
        strict = rows > cols
        upper = rows <= cols
        ones = jnp.ones((rows_n, rows_n), BF16)
    lower_b = jnp.where(causal, 1.0, 0.0).astype(BF16)
    upper_b = jnp.where(upper, 1.0, 0.0).astype(BF16)
    eye = jnp.where(rows == cols, 1.0, 0.0)

    beta, g = _gates(ba, alog, dtb)
    g0 = g.astype(BF16)
    r1 = g - g0.astype(F32)
    g1 = r1.astype(BF16)
    g2 = (r1 - g1.astype(F32)).astype(BF16)
    gc = _dot(lower_b, g0) + (_dot(lower_b, g1) + _dot(lower_b, g2))
    gct = _dot_tn(g0, upper_b) + (_dot_tn(g1, upper_b) + _dot_tn(g2, upper_b))
    gl = _dot(ones, g0) + (_dot(ones, g1) + _dot(ones, g2))
    yield

    def l2n(x, scale):
        return x * (lax.rsqrt(jnp.sum(x * x, axis=-1, keepdims=True) + EPS) * scale)

    qn = [l2n(qkv[:, h * dk:(h + 1) * dk], dk ** -0.5) for h in hs]
    kn = [l2n(qkv[:, dd + h * dk:dd + (h + 1) * dk], 1.0) for h in hs]
    vv = [qkv[:, 2 * dd + h * dk:2 * dd + (h + 1) * dk] for h in hs]
    bcol = [beta[:, h:h + 1] for h in hs]
    gcol = [gc[:, heads + h:heads + h + 1] for h in hs]
    glc = [gl[:, heads + h:heads + h + 1] for h in hs]
    decay = [jnp.where(causal, jnp.exp(jnp.where(causal, gcol[h] - gct[heads + h:heads + h + 1, :], 0.0)), 0.0)
             for h in hs]
    kb = [kn[h].astype(BF16) for h in hs]
    kq = [jnp.concatenate([kn[h], qn[h]], axis=0).astype(BF16) for h in hs]
    kkqk = [_dot_nt(kq[h], kb[h]) for h in hs]
    yield
    low = [jnp.where(strict, bcol[h] * kkqk[h][:rows_n] * decay[h], 0.0) for h in hs]
    attn = [(kkqk[h][rows_n:] * decay[h]).astype(BF16) for h in hs]
    inv = [eye - jnp.where((rows ^ cols) == 1, low[h], 0.0) for h in hs]
    b = 2
    while b < c:
        pair = ((rows ^ cols) >= b) & ((rows ^ cols) < 2 * b)
        off = [jnp.where(pair, low[h], 0.0).astype(BF16) for h in hs]
        invb = [inv[h].astype(BF16) for h in hs]
        y = [_dot(off[h], invb[h]).astype(BF16) for h in hs]
        yield
        inv = [inv[h] - _dot(invb[h], y[h]) for h in hs]
        yield
        b *= 2
    egc = [jnp.exp(gcol[h]) for h in hs]
    rhs = [jnp.concatenate([bcol[h] * vv[h], (bcol[h] * egc[h]) * kn[h]], axis=1).astype(BF16) for h in hs]
    sol = [_dot(inv[h].astype(BF16), rhs[h]) for h in hs]
    yield
    qg = [qn[h] * egc[h] for h in hs]
    kdb = [(kn[h] * jnp.exp(glc[h] - gcol[h])).astype(BF16) for h in hs]

    if nseq == 1:
        sb = [s_prev[0][h].astype(BF16) for h in hs]
        wq = [jnp.concatenate([sol[h][:, dk:], qg[h]], axis=0).astype(BF16) for h in hs]
        ys = [_dot(wq[h], sb[h]) for h in hs]
        yield
        ub = [(sol[h][:, :dk] - ys[h][:rows_n]).astype(BF16) for h in hs]
        o = [ys[h][rows_n:] + _dot(attn[h], ub[h]) for h in hs]
        yield
        s_new = [[jnp.exp(glc[h][0:1, :]) * s_prev[0][h] + _dot_tn(kdb[h], ub[h]) for h in hs]]
    else:
        js = range(nseq)
        sb = [[s_prev[j][h].astype(BF16) for h in hs] for j in js]
        wq = [[jnp.concatenate([sol[h][j * c:(j + 1) * c, dk:], qg[h][j * c:(j + 1) * c]], axis=0).astype(BF16)
               for h in hs] for j in js]
        ys = [[_dot(wq[j][h], sb[j][h]) for h in hs] for j in js]
        ws = [jnp.concatenate([ys[j][h][:c] for j in js], axis=0) for h in hs]
        qs = [jnp.concatenate([ys[j][h][c:] for j in js], axis=0) for h in hs]
        u = [sol[h][:, :dk] - ws[h] for h in hs]
        o = [qs[h] + _dot(attn[h], u[h].astype(BF16)) for h in hs]
        seq_of_row = lax.broadcasted_iota(jnp.int32, (rows_n, dk), 0) >> (c.bit_length() - 1)
        s_new = [[None] * heads for _ in js]
        for j in range(0, nseq, 2):
            um = [jnp.concatenate([jnp.where(seq_of_row == j, u[h], 0.0),
                                   jnp.where(seq_of_row == j + 1, u[h], 0.0)], axis=1).astype(BF16) for h in hs]
            sn = [_dot_tn(kdb[h], um[h]) for h in hs]
            for h in hs:
                s_new[j][h] = jnp.exp(glc[h][j * c:j * c + 1, :]) * s_prev[j][h] + sn[h][:, :dk]
                s_new[j + 1][h] = (jnp.exp(glc[h][(j + 1) * c:(j + 1) * c + 1, :]) * s_prev[j + 1][h]
                                   + sn[h][:, dk:])
    out = []
    for h in hs:
        on = o[h] * lax.rsqrt(jnp.mean(o[h] * o[h], axis=-1, keepdims=True) + EPS) * hng
        out.append(on * _silu(z_ref[:, h * dk:(h + 1) * dk]))
    return out, s_new


def _lockstep(gens):
    results = [None] * len(gens)
    live = list(range(len(gens)))
    while live:
        for idx in list(live):
            try:
                next(gens[idx])
            except StopIteration as stop:
                results[idx] = stop.value
                live.remove(idx)
    return results


def _delta_prompt_kernel(qkv_ref, z_ref, ba_ref, wconv_ref, alog_ref, dtb_ref, hng_ref,
                         o_ref, s_ref, qbuf_ref, ext_ref, *, c, nb, heads, dk):
    i = pl.program_id(1)

    @pl.when(i == 0)
    def _():
        ext_ref[...] = jnp.zeros(ext_ref.shape, F32)
        s_ref[...] = jnp.zeros(s_ref.shape, F32)

    gens = []
    for g in range(nb):
        xe = jnp.concatenate([ext_ref[g], qkv_ref[g]], axis=0)
        tiles = xe.reshape((QKV_HIST + c) // SUBLANE, SUBLANE, xe.shape[1])
        acc = _short_conv(tiles, lambda a: a[1:], lambda a: a[:-1], wconv_ref).reshape(c, xe.shape[1])
        ext_ref[g] = qkv_ref[g, c - QKV_HIST:c, :]
        s_prev = [[s_ref[g, h] for h in range(heads)]]
        gens.append(_delta_rows(_silu(acc), z_ref.at[g], ba_ref[g], alog_ref[0], dtb_ref[0], hng_ref[0],
                                s_prev, c=c, nseq=1, heads=heads, dk=dk))
    for g, (out, s_new) in enumerate(_lockstep(gens)):
        for h in range(heads):
            o_ref[g, :, h * dk:(h + 1) * dk] = out[h].astype(o_ref.dtype)
            s_ref[g, h] = s_new[0][h]

    @pl.when(i == pl.num_programs(1) - 1)
    def _():
        qbuf_ref[...] = qkv_ref[:, c - (QKV_CONV - 1):c, :]


def _layer_spec(arr, layer):
    zeros = (0,) * (arr.ndim - 1)
    return pl.BlockSpec((1,) + arr.shape[1:], lambda *_: (layer,) + zeros)


def _delta_prompt(proj, ba, wconv, alog, dtb, hng, layer, *, b, t, nb, heads, dk):
    dd = heads * dk
    c = min(CHUNK, t)
    nc = t // c
    proj3 = proj.reshape(b, t, proj.shape[1])
    kern = functools.partial(_delta_prompt_kernel, c=c, nb=nb, heads=heads, dk=dk)
    o, s_new, qbuf = pl.pallas_call(
        kern,
        out_shape=(jax.ShapeDtypeStruct((b, t, dd), BF16),
                   jax.ShapeDtypeStruct((b, heads, dk, dk), F32),
                   jax.ShapeDtypeStruct((b, QKV_CONV - 1, 3 * dd), F32)),
        grid=(b // nb, nc),
        in_specs=[pl.BlockSpec((nb, c, 3 * dd), lambda i, j: (i, j, 0)),
                  pl.BlockSpec((nb, c, dd), lambda i, j: (i, j, 3)),
                  pl.BlockSpec((nb, c, LANE), lambda i, j: (i, j, 0)),
                  _layer_spec(wconv, layer), _layer_spec(alog, layer), _layer_spec(dtb, layer),
                  _layer_spec(hng, layer)],
        out_specs=(pl.BlockSpec((nb, c, dd), lambda i, j: (i, j, 0)),
                   pl.BlockSpec((nb, heads, dk, dk), lambda i, j: (i, 0, 0, 0)),
                   pl.BlockSpec((nb, QKV_CONV - 1, 3 * dd), lambda i, j: (i, 0, 0))),
        scratch_shapes=[pltpu.VMEM((nb, QKV_HIST, 3 * dd), F32)],
        compiler_params=_params(2),
        name="delta_prompt",
    )(proj3, proj3, ba.reshape(b, t, LANE), wconv, alog, dtb, hng)
    return o.reshape(b * t, dd), s_new, qbuf


def _write_slab(ref, slab, value):
    for other in range(ref.shape[0]):
        if other != slab:
            ref[other] = jnp.zeros(ref.shape[1:], ref.dtype)
    ref[slab] = value


def _delta_sample_kernel(qkv_ref, z_ref, ba_ref, hist_ref, s0_ref, wconv_ref, alog_ref, dtb_ref,
                         hng_ref, *rest, t, nseq, heads, dk, slab):
    o_ref, s_ref, qbuf_ref, ext_ref = rest[-4:]
    n = ext_ref.shape[2]
    first = QKV_HIST - (QKV_CONV - 1)
    ext_ref[:, 0:first, :] = jnp.zeros((nseq, first, n), F32)
    for r in range(QKV_CONV - 1):
        ext_ref[:, first + r, :] = hist_ref[0, r]
    ext_ref[:, QKV_HIST:QKV_HIST + t, :] = qkv_ref[...].reshape(nseq, t, n)
    per_seq = (QKV_HIST + t) // SUBLANE
    tiles = ext_ref[...].reshape(nseq * per_seq, SUBLANE, n)

    def pick(a, lo, hi):
        return a.reshape(nseq, per_seq, SUBLANE, n)[:, lo:hi].reshape(nseq * (per_seq - 1), SUBLANE, n)

    acc = _short_conv(tiles, lambda a: pick(a, 1, per_seq), lambda a: pick(a, 0, per_seq - 1),
                      wconv_ref).reshape(nseq * t, n)
    _write_slab(qbuf_ref, slab, jnp.stack(
        [ext_ref[:, QKV_HIST + t - (QKV_CONV - 1) + r, :] for r in range(QKV_CONV - 1)]))

    s_prev = [[s0_ref[0, j, h] for h in range(heads)] for j in range(nseq)]
    (out, s_new), = _lockstep([_delta_rows(_silu(acc), z_ref, ba_ref[...], alog_ref[0], dtb_ref[0],
                                           hng_ref[0], s_prev, c=t, nseq=nseq, heads=heads, dk=dk)])
    for other in range(s_ref.shape[0]):
        if other != slab:
            s_ref[other] = jnp.zeros(s_ref.shape[1:], F32)
    for h in range(heads):
        o_ref[:, h * dk:(h + 1) * dk] = out[h].astype(o_ref.dtype)
        for j in range(nseq):
            s_ref[slab, j, h] = s_new[j][h]


def _layered_out(depth, layer, prev, block_tail, n_in, batch_pos=0):
    def index(first, i):
        tail = [0] * len(block_tail)
        tail[batch_pos] = i
        return (first,) + tuple(tail)

    if prev is None:
        return pl.BlockSpec((depth,) + block_tail, lambda i: index(0, i)), layer, None
    return pl.BlockSpec((1,) + block_tail, lambda i: index(layer, i)), 0, n_in


def _delta_sample(proj, ba, hist, state, prev, layer, wconv, alog, dtb, hng, *, b, t, nseq, heads, dk):
    dd = heads * dk
    depth = state.shape[0]
    rows_n = nseq * t
    in_specs = [pl.BlockSpec((rows_n, 3 * dd), lambda i: (i, 0)),
                pl.BlockSpec((rows_n, dd), lambda i: (i, 3)),
                pl.BlockSpec((rows_n, LANE), lambda i: (i, 0)),
                pl.BlockSpec((1, QKV_CONV - 1, nseq, 3 * dd), lambda i: (layer, 0, i, 0)),
                pl.BlockSpec((1, nseq, heads, dk, dk), lambda i: (layer, i, 0, 0, 0)),
                _layer_spec(wconv, layer), _layer_spec(alog, layer), _layer_spec(dtb, layer),
                _layer_spec(hng, layer)]
    args = [proj, proj, ba, hist, state, wconv, alog, dtb, hng]
    s_spec, slab, alias_at = _layered_out(depth, layer, prev, (nseq, heads, dk, dk), len(args))
    q_spec, _, _ = _layered_out(depth, layer, prev, (QKV_CONV - 1, nseq, 3 * dd), len(args), batch_pos=1)
    aliases = {}
    if prev is not None:
        in_specs += [pl.BlockSpec(memory_space=pl.ANY)] * 2
        args += list(prev)
        aliases = {alias_at: 1, alias_at + 1: 2}
    kern = functools.partial(_delta_sample_kernel, t=t, nseq=nseq, heads=heads, dk=dk, slab=slab)
    return pl.pallas_call(
        kern,
        out_shape=(jax.ShapeDtypeStruct((b * t, dd), BF16),
                   jax.ShapeDtypeStruct((depth, b, heads, dk, dk), F32),
                   jax.ShapeDtypeStruct((depth, QKV_CONV - 1, b, 3 * dd), F32)),
        grid=(b // nseq,),
        in_specs=in_specs,
        out_specs=(pl.BlockSpec((rows_n, dd), lambda i: (i, 0)), s_spec, q_spec),
        scratch_shapes=[pltpu.VMEM((nseq, QKV_HIST + t, 3 * dd), F32)],
        input_output_aliases=aliases,
        compiler_params=_params(1),
        name="delta_sample",
    )(*args)


def _conf_epilogue(conv, zc, bdw, lng, lnb):
    y = conv + bdw
    mu = jnp.mean(y, axis=-1, keepdims=True)
    yc = y - mu
    var = jnp.mean(yc * yc, axis=-1, keepdims=True)
    y = yc * lax.rsqrt(var + EPS) * lng + lnb
    return _silu(y) * _silu(zc)


def _conf_tile(ga_ref, gb_ref, zc_ref, w_ref, bdw_ref, lng_ref, lnb_ref, ext_ref, sh_ref, conv_ref,
               *, tt, rr):
    dc = ext_ref.shape[1]
    first = GLU_HIST - (CONF_KERNEL - 1)
    ext_ref[GLU_HIST:GLU_HIST + tt, :] = _glu(ga_ref[...], gb_ref[...])
    for p in range(SUBLANE):
        for lt in range(dc // LANE):
            sh_ref[p, lt] = ext_ref[p:p + GLU_HIST + tt, lt * LANE:(lt + 1) * LANE]
    for lt in range(dc // LANE):
        lanes = slice(lt * LANE, (lt + 1) * LANE)
        taps = [jnp.broadcast_to(w_ref[0, j:j + 1, lanes], (rr, LANE)) for j in range(CONF_KERNEL)]

        def body(r, carry, lt=lt, lanes=lanes, taps=taps):
            r0 = pl.multiple_of(r * rr, rr)
            parts = [None] * CONF_PARTIALS
            for j in range(CONF_KERNEL):
                off = first + j
                win = sh_ref[off % SUBLANE, lt, pl.ds(r0 + (off // SUBLANE) * SUBLANE, rr), :]
                p = j % CONF_PARTIALS
                parts[p] = taps[j] * win if parts[p] is None else parts[p] + taps[j] * win
            acc = parts[0]
            for p in range(1, CONF_PARTIALS):
                acc = acc + parts[p]
            conv_ref[pl.ds(r0, rr), lanes] = acc
            return carry

        lax.fori_loop(0, tt // rr, body, 0)
    y = _conf_epilogue(conv_ref[...], zc_ref[...], bdw_ref[0], lng_ref[0], lnb_ref[0])
    ext_ref[0:GLU_HIST, :] = ext_ref[tt:tt + GLU_HIST, :]
    return y


def _conf_prompt_kernel(ga_ref, gb_ref, zc_ref, w_ref, bdw_ref, lng_ref, lnb_ref,
                        u_ref, gbuf_ref, ext_ref, sh_ref, conv_ref, *, tt, rr):
    i = pl.program_id(1)
    dc = ext_ref.shape[1]

    @pl.when(i == 0)
    def _():
        ext_ref[0:GLU_HIST, :] = jnp.zeros((GLU_HIST, dc), F32)
        ext_ref[GLU_HIST + tt:GLU_HIST + tt + SUBLANE, :] = jnp.zeros((SUBLANE, dc), F32)

    y = _conf_tile(ga_ref, gb_ref, zc_ref, w_ref, bdw_ref, lng_ref, lnb_ref, ext_ref, sh_ref, conv_ref,
                   tt=tt, rr=rr)
    u_ref[...] = y.astype(u_ref.dtype)

    @pl.when(i == pl.num_programs(1) - 1)
    def _():
        gbuf_ref[0] = ext_ref[GLU_HIST - (CONF_KERNEL - 1):GLU_HIST, :]


def _conf_prompt(proj, w_dw, b_dw, ln_g, ln_b, layer, *, b, t, dd, dc, tt):
    nt = t // tt
    rr = min(32, tt)
    col0 = (4 * dd) // dc
    kern = functools.partial(_conf_prompt_kernel, tt=tt, rr=rr)
    row = lambda i, j: i * nt + j
    return pl.pallas_call(
        kern,
        out_shape=(jax.ShapeDtypeStruct((b * t, dc), BF16),
                   jax.ShapeDtypeStruct((b, CONF_KERNEL - 1, dc), F32)),
        grid=(b, nt),
        in_specs=[pl.BlockSpec((tt, dc), lambda i, j: (row(i, j), col0)),
                  pl.BlockSpec((tt, dc), lambda i, j: (row(i, j), col0 + 1)),
                  pl.BlockSpec((tt, dc), lambda i, j: (row(i, j), col0 + 2)),
                  _layer_spec(w_dw, layer), _layer_spec(b_dw, layer), _layer_spec(ln_g, layer),
                  _layer_spec(ln_b, layer)],
        out_specs=(pl.BlockSpec((tt, dc), lambda i, j: (row(i, j), 0)),
                   pl.BlockSpec((1, CONF_KERNEL - 1, dc), lambda i, j: (i, 0, 0))),
        scratch_shapes=[pltpu.VMEM((GLU_HIST + tt + SUBLANE, dc), F32),
                        pltpu.VMEM((SUBLANE, dc // LANE, GLU_HIST + tt, LANE), F32),
                        pltpu.VMEM((tt, dc), F32)],
        compiler_params=_params(2),
        name="conf_prompt",
    )(proj, proj, proj, w_dw, b_dw, ln_g, ln_b)


def _conf_sample_kernel(ga_ref, gb_ref, zc_ref, hist_ref, w_ref, bdw_ref, lng_ref, lnb_ref,
                        *rest, t, bb, slab):
    u_ref, gbuf_ref, ext_ref, conv_ref, glu_ref = rest[-5:]
    dc = conv_ref.shape[2]
    hist = CONF_KERNEL - 1
    for other in range(gbuf_ref.shape[0]):
        if other != slab:
            gbuf_ref[other] = jnp.zeros(gbuf_ref.shape[1:], F32)
    glu_ref[...] = _glu(ga_ref[...], gb_ref[...])
    for lt in range(dc // LANE):
        lanes = slice(lt * LANE, (lt + 1) * LANE)
        ext_ref[lt, 0:hist] = hist_ref[0, :, :, lanes]
        for step in range(t):
            ext_ref[lt, hist + step] = glu_ref[:, step, lanes]
        acc = None
        for j in range(CONF_KERNEL):
            term = w_ref[0, j:j + 1, lanes] * ext_ref[lt, j:j + t]
            acc = term if acc is None else acc + term
        conv_ref[:, :, lanes] = acc
        gbuf_ref[slab, :, :, lanes] = ext_ref[lt, t:t + hist]
    zc = jnp.stack([zc_ref[:, step, :] for step in range(t)])
    y = _conf_epilogue(conv_ref[...], zc, bdw_ref[0], lng_ref[0], lnb_ref[0])
    for step in range(t):
        u_ref[:, step, :] = y[step]


def _conf_sample(proj, hist, prev, layer, w_dw, b_dw, ln_g, ln_b, *, b, t, dd, dc, bb):
    depth = hist.shape[0]
    col0 = (4 * dd) // dc
    proj = proj.reshape(b, t, proj.shape[1])
    in_specs = [pl.BlockSpec((bb, t, dc), lambda i: (i, 0, col0)),
                pl.BlockSpec((bb, t, dc), lambda i: (i, 0, col0 + 1)),
                pl.BlockSpec((bb, t, dc), lambda i: (i, 0, col0 + 2)),
                pl.BlockSpec((1, CONF_KERNEL - 1, bb, dc), lambda i: (layer, 0, i, 0)),
                _layer_spec(w_dw, layer), _layer_spec(b_dw, layer), _layer_spec(ln_g, layer),
                _layer_spec(ln_b, layer)]
    args = [proj, proj, proj, hist, w_dw, b_dw, ln_g, ln_b]
    g_spec, slab, alias_at = _layered_out(depth, layer, prev, (CONF_KERNEL - 1, bb, dc), len(args),
                                          batch_pos=1)
    aliases = {}
    if prev is not None:
        in_specs.append(pl.BlockSpec(memory_space=pl.ANY))
        args.append(prev)
        aliases = {alias_at: 1}
    kern = functools.partial(_conf_sample_kernel, t=t, bb=bb, slab=slab)
    u, gbuf = pl.pallas_call(
        kern,
        out_shape=(jax.ShapeDtypeStruct((b, t, dc), F32),
                   jax.ShapeDtypeStruct((depth, CONF_KERNEL - 1, b, dc), F32)),
        grid=(b // bb,),
        in_specs=in_specs,
        out_specs=(pl.BlockSpec((bb, t, dc), lambda i: (i, 0, 0)), g_spec),
        scratch_shapes=[pltpu.VMEM((dc // LANE, CONF_KERNEL - 1 + t, bb, LANE), F32),
                        pltpu.VMEM((t, bb, dc), F32),
                        pltpu.VMEM((bb, t, dc), F32)],
        input_output_aliases=aliases,
        compiler_params=_params(1),
        name="conf_sample",
    )(*args)
    return u.reshape(b * t, dc), gbuf


def _outproj_kernel(x_ref, gate_ref, o_ref, u_ref, wo_ref, wu_ref, fg_ref, y_ref, *, final):
    mix = (_dot(o_ref[...], wo_ref[0].astype(BF16))
           + _dot(u_ref[...].astype(BF16), wu_ref[0].astype(BF16)))
    x = x_ref[...]
    y = x + gate_ref[...] * mix.reshape(x.shape)
    if final:
        y = y * lax.rsqrt(jnp.mean(y * y, axis=-1, keepdims=True) + EPS) * fg_ref[...]
    y_ref[...] = y


def _outproj(x, mod, o, u, w_out, final_g, layer, *, bb, tt, final):
    b, t, d = x.shape
    dd, dc = o.shape[1], u.shape[1]
    nt = t // tt
    tm = bb * tt
    kern = functools.partial(_outproj_kernel, final=final)
    return pl.pallas_call(
        kern,
        out_shape=jax.ShapeDtypeStruct((b, t, d), F32),
        grid=(b // bb, nt),
        in_specs=[pl.BlockSpec((bb, tt, d), lambda i, j: (i, j, 0)),
                  pl.BlockSpec((bb, 1, d), lambda i, j: (i, 0, 2)),
                  pl.BlockSpec((tm, dd), lambda i, j: (i * nt + j, 0)),
                  pl.BlockSpec((tm, dc), lambda i, j: (i * nt + j, 0)),
                  pl.BlockSpec((1, dd, d), lambda i, j: (layer, 0, 0),
                               pipeline_mode=pl.Buffered(1)),
                  pl.BlockSpec((1, dc, d), lambda i, j: (layer, dd // dc, 0),
                               pipeline_mode=pl.Buffered(1)),
                  pl.BlockSpec((1, d), lambda i, j: (0, 0))],
        out_specs=pl.BlockSpec((bb, tt, d), lambda i, j: (i, j, 0)),
        compiler_params=_params(2),
        name="out_proj",
    )(x, mod, o, u, w_out, w_out, final_g.reshape(1, d))


def kernel(x_prompt, x_sample, c_prompt, c_sample, state_delta, state_qkv_conv, state_glu_conv,
           norm_g, w_ada, b_ada, w_in, w_qkv_conv, a_log, dt_bias, head_norm_g, w_dw, b_dw,
           ln_g, ln_b, w_out, final_g):
    depth = w_in.shape[0]
    bp, tp, d = x_prompt.shape
    bs, ts, _ = x_sample.shape
    heads = a_log.shape[1]
    dd = w_qkv_conv.shape[2] // 3
    dk = dd // heads
    dc = w_dw.shape[2]
    assert ts % SUBLANE == 0 and ts >= QKV_CONV - 1 and ts <= CONF_KERNEL - 1
    assert 2 * heads <= LANE and dd % dc == 0 and tp % CHUNK == 0
    assert CHUNK % ts == 0 and bs % (CHUNK // ts) == 0 and (CHUNK // ts) % 2 == 0

    mod = _ada(jnp.concatenate([c_prompt, c_sample], axis=0), w_ada, b_ada)

    tn = min(1024, dd, dc)
    tm = 1024
    tt_in = min(tm, tp)
    bb_s = min(bs, tm // ts)
    tt_out = min(512, tp)
    bb_out = min(bs, 512 // ts)
    tt_conf = min(256, tp)
    bb_conf = min(bs, 16)
    nb_delta = 4 if bp % 4 == 0 else 1
    nseq = CHUNK // ts

    w_t = jnp.swapaxes(w_in, 1, 2)
    w_ba = jnp.pad(w_t[:, 4 * dd:4 * dd + 2 * heads], ((0, 0), (0, LANE - 2 * heads), (0, 0)))
    w_out_b = w_out
    qkv_hist = jnp.swapaxes(state_qkv_conv, 1, 2)
    glu_hist = jnp.swapaxes(state_glu_conv, 1, 2)
    norm_g3 = norm_g.reshape(depth, 1, d)
    alog = jnp.pad(a_log, ((0, 0), (heads, LANE - 2 * heads))).reshape(depth, 1, LANE)
    dtb = jnp.pad(dt_bias, ((0, 0), (heads, LANE - 2 * heads))).reshape(depth, 1, LANE)
    hng = head_norm_g.reshape(depth, 1, dk)
    bdw, lng, lnb = b_dw.reshape(depth, 1, dc), ln_g.reshape(depth, 1, dc), ln_b.reshape(depth, 1, dc)

    hp, hs = x_prompt, x_sample
    sp_l, qp_l, gp_l = [], [], []
    s_s = q_s = g_s = None
    for l in range(depth):
        mod_p = mod[l, :bp].reshape(bp, 1, 3 * d)
        mod_s = mod[l, bp:].reshape(bs, 1, 3 * d)
        final = l == depth - 1

        proj, ba = _inproj(hp, mod_p, norm_g3, w_t, w_ba, l, bb=1, tt=tt_in, tn=tn,
                           n_a=4 * dd // tn, skip=2 * heads)
        o, s_new, qbuf = _delta_prompt(proj, ba, w_qkv_conv, alog, dtb, hng, l,
                                       b=bp, t=tp, nb=nb_delta, heads=heads, dk=dk)
        u, gbuf = _conf_prompt(proj, w_dw, bdw, lng, lnb, l, b=bp, t=tp, dd=dd, dc=dc, tt=tt_conf)
        hp = _outproj(hp, mod_p, o, u, w_out_b, final_g, l, bb=1, tt=tt_out, final=final)
        sp_l.append(s_new); qp_l.append(qbuf); gp_l.append(gbuf)

        proj, ba = _inproj(hs, mod_s, norm_g3, w_t, w_ba, l, bb=bb_s, tt=ts, tn=tn,
                           n_a=4 * dd // tn, skip=2 * heads)
        o, s_s, q_s = _delta_sample(proj, ba, qkv_hist, state_delta,
                                    None if l == 0 else (s_s, q_s), l, w_qkv_conv, alog, dtb, hng,
                                    b=bs, t=ts, nseq=nseq, heads=heads, dk=dk)
        u, g_s = _conf_sample(proj, glu_hist, g_s, l, w_dw, bdw, lng, lnb,
                              b=bs, t=ts, dd=dd, dc=dc, bb=bb_conf)
        hs = _outproj(hs, mod_s, o, u, w_out_b, final_g, l, bb=bb_out, tt=ts, final=final)

    return (hp, hs, jnp.stack(sp_l), jnp.stack(qp_l), jnp.stack(gp_l),
            s_s, jnp.swapaxes(q_s, 1, 2), jnp.swapaxes(g_s, 1, 2))
```

```python
import functools
import math

import jax
import jax.numpy as jnp
from jax import lax
from jax.experimental import pallas as pl
from jax.experimental.pallas import tpu as pltpu

F32 = jnp.float32
BF16 = jnp.bfloat16

EPS = 1e-6
QKV_CONV = 4
CONF_KERNEL = 31
CHUNK = 64
LANE = 128
SUBLANE = 8
QKV_HIST = SUBLANE
GLU_HIST = 32
CONF_PARTIALS = 3
VMEM_LIMIT_BYTES = 56 * 1024 * 1024


def _params(n_axes):
    return pltpu.CompilerParams(dimension_semantics=("arbitrary",) * n_axes,
                                vmem_limit_bytes=VMEM_LIMIT_BYTES)


def _sigmoid(x):
    return 0.5 * jnp.tanh(0.5 * x) + 0.5


def _glu(a, b):
    half = 0.5 * a
    return half * jnp.tanh(0.5 * b) + half


def _silu(x):
    half = 0.5 * x
    return half * jnp.tanh(half) + half


def _softplus(x):
    return jnp.maximum(x, 0.0) + jnp.log1p(jnp.exp(-jnp.abs(x)))


def _mod_spec(mod, bb, d, col):
    if mod.ndim == 2:
        return pl.BlockSpec((bb, d), lambda i, *_: (i, col))
    return pl.BlockSpec((bb, 1, d), lambda i, *_: (i, 0, col))


def _per_sequence(ref):
    v = ref[...]
    return v if v.ndim == 3 else v[:, None, :]


def _dot(a, b):
    return jnp.dot(a, b, preferred_element_type=F32)


def _dot_nt(a, b):
    return lax.dot_general(a, b, (((1,), (1,)), ((), ())), preferred_element_type=F32)


def _dot_tn(a, b):
    return lax.dot_general(a, b, (((0,), (0,)), ((), ())), preferred_element_type=F32)


def _ada_kernel(c_ref, w_ref, b_ref, o_ref):
    s = _silu(c_ref[...]).astype(BF16)
    o_ref[0] = _dot(s, w_ref[0].astype(BF16)) + b_ref[0]


def _ada(c_all, w_ada, b_ada):
    depth, d, n = w_ada.shape
    rows = c_all.shape[0]
    tn = n // 4 if n % (4 * LANE) == 0 else n
    return pl.pallas_call(
        _ada_kernel,
        out_shape=jax.ShapeDtypeStruct((depth, rows, n), F32),
        grid=(depth, n // tn),
        in_specs=[pl.BlockSpec((rows, d), lambda l, j: (0, 0)),
                  pl.BlockSpec((1, d, tn), lambda l, j: (l, 0, j)),
                  pl.BlockSpec((1, 1, tn), lambda l, j: (l, 0, j))],
        out_specs=pl.BlockSpec((1, rows, tn), lambda l, j: (l, 0, j)),
        compiler_params=_params(2),
        name="ada_mod",
    )(c_all, w_ada, b_ada.reshape(depth, 1, n))


def _inproj_kernel(x_ref, shift_ref, scale_ref, g_ref, w_ref, wba_ref, proj_ref, ba_ref, h_ref):
    @pl.when(pl.program_id(2) == 0)
    def _():
        x = x_ref[...]
        ms = jnp.mean(x * x, axis=-1, keepdims=True)
        gain = g_ref[0] * (1.0 + _per_sequence(scale_ref))
        h = x * lax.rsqrt(ms + EPS) * gain + _per_sequence(shift_ref)
        h = h.reshape(h_ref.shape).astype(BF16)
        h_ref[...] = h
        ba_ref[...] = _dot_nt(h, wba_ref[0].astype(BF16))

    proj_ref[...] = _dot_nt(h_ref[...], w_ref[0].astype(BF16))


def _inproj(x, mod, norm_g, w_t, w_ba, layer, *, bb, tt, tn, n_a, skip):
    b, t, d = x.shape
    n = w_t.shape[1] - skip
    nb, nt, nn = b // bb, t // tt, n // tn
    tm = bb * tt
    return pl.pallas_call(
        _inproj_kernel,
        out_shape=(jax.ShapeDtypeStruct((b * t, n), F32),
                   jax.ShapeDtypeStruct((b * t, LANE), F32)),
        grid=(nb, nt, nn),
        in_specs=[pl.BlockSpec((bb, tt, d), lambda i, j, k: (i, j, 0)),
                  _mod_spec(mod, bb, d, 0), _mod_spec(mod, bb, d, 1),
                  pl.BlockSpec((1, 1, d), lambda i, j, k: (layer, 0, 0)),
                  pl.BlockSpec((pl.Element(1), pl.Element(tn), pl.Element(d)),
                               lambda i, j, k: (layer, pl.multiple_of(
                                   k * tn + jnp.where(k >= n_a, skip, 0), math.gcd(tn, skip)), 0)),
                  pl.BlockSpec((1, LANE, d), lambda i, j, k: (layer, 0, 0))],
        out_specs=(pl.BlockSpec((tm, tn), lambda i, j, k: (i * nt + j, k)),
                   pl.BlockSpec((tm, LANE), lambda i, j, k: (i * nt + j, 0))),
        scratch_shapes=[pltpu.VMEM((tm, d), BF16)],
        compiler_params=_params(3),
        name="in_proj",
    )(x, mod, mod, norm_g, w_t, w_ba)


def _short_conv(tiles, pick_cur, pick_prev, wconv_ref):
    cur, prev = pick_cur(tiles), pick_prev(tiles)
    sub = lax.broadcasted_iota(jnp.int32, cur.shape, cur.ndim - 2)
    acc = wconv_ref[0, QKV_CONV - 1:QKV_CONV, :] * cur
    for d in range(1, QKV_CONV):
        delayed = pltpu.roll(jnp.where(sub < SUBLANE - d, cur, prev), d, axis=1)
        acc = acc + wconv_ref[0, QKV_CONV - 1 - d:QKV_CONV - d, :] * delayed
    return acc


def _gates(ba, alog, dtb):
    beta = _sigmoid(ba)
    g = -jnp.exp(alog) * _softplus(ba + dtb)
    return beta, g


def _delta_rows(qkv, z_ref, ba, alog, dtb, hng, s_prev, *, c, nseq, heads, dk):
    rows_n = nseq * c
    dd = heads * dk
    hs = range(heads)
    rows = lax.broadcasted_iota(jnp.int32, (rows_n, rows_n), 0)
    cols = lax.broadcasted_iota(jnp.int32, (rows_n, rows_n), 1)
    if nseq > 1:
        shift = c.bit_length() - 1
        same = (rows >> shift) == (cols >> shift)
        causal = same & (rows >= cols)
        strict = same & (rows > cols)
        upper = same & (rows <= cols)
        ones = jnp.where(same, 1.0, 0.0).astype(BF16)
    else:
        causal = rows >= cols
        strict = rows > cols
        upper = rows <= cols
        ones = jnp.ones((rows_n, rows_n), BF16)
    lower_b = jnp.where(causal, 1.0, 0.0).astype(BF16)
    upper_b = jnp.where(upper, 1.0, 0.0).astype(BF16)
    eye = jnp.where(rows == cols, 1.0, 0.0)

    beta, g = _gates(ba, alog, dtb)
    g0 = g.astype(BF16)
    r1 = g - g0.astype(F32)
    g1 = r1.astype(BF16)
    g2 = (r1 - g1.astype(F32)).astype(BF16)
    gc = _dot(lower_b, g0) + (_dot(lower_b, g1) + _dot(lower_b, g2))
    gct = _dot_tn(g0, upper_b) + (_dot_tn(g1, upper_b) + _dot_tn(g2, upper_b))
    gl = _dot(ones, g0) + (_dot(ones, g1) + _dot(ones, g2))
    yield

    def l2n(x, scale):
        return x * (lax.rsqrt(jnp.sum(x * x, axis=-1, keepdims=True) + EPS) * scale)

    qn = [l2n(qkv[:, h * dk:(h + 1) * dk], dk ** -0.5) for h in hs]
    kn = [l2n(qkv[:, dd + h * dk:dd + (h + 1) * dk], 1.0) for h in hs]
    vv = [qkv[:, 2 * dd + h * dk:2 * dd + (h + 1) * dk] for h in hs]
    bcol = [beta[:, h:h + 1] for h in hs]
    gcol = [gc[:, heads + h:heads + h + 1] for h in hs]
    glc = [gl[:, heads + h:heads + h + 1] for h in hs]
    decay = [jnp.where(causal, jnp.exp(jnp.where(causal, gcol[h] - gct[heads + h:heads + h + 1, :], 0.0)), 0.0)
             for h in hs]
    kb = [kn[h].astype(BF16) for h in hs]
    kq = [jnp.concatenate([kn[h], qn[h]], axis=0).astype(BF16) for h in hs]
    kkqk = [_dot_nt(kq[h], kb[h]) for h in hs]
    yield
    low = [jnp.where(strict, bcol[h] * kkqk[h][:rows_n] * decay[h], 0.0) for h in hs]
    attn = [(kkqk[h][rows_n:] * decay[h]).astype(BF16) for h in hs]
    inv = [eye - jnp.where((rows ^ cols) == 1, low[h], 0.0) for h in hs]
    b = 2
    while b < c:
        pair = ((rows ^ cols) >= b) & ((rows ^ cols) < 2 * b)
        off = [jnp.where(pair, low[h], 0.0).astype(BF16) for h in hs]
        invb = [inv[h].astype(BF16) for h in hs]
        y = [_dot(off[h], invb[h]).astype(BF16) for h in hs]
        yield
        inv = [inv[h] - _dot(invb[h], y[h]) for h in hs]
        yield
        b *= 2
    egc = [jnp.exp(gcol[h]) for h in hs]
    rhs = [jnp.concatenate([bcol[h] * vv[h], (bcol[h] * egc[h]) * kn[h]], axis=1).astype(BF16) for h in hs]
    sol = [_dot(inv[h].astype(BF16), rhs[h]) for h in hs]
    yield
    qg = [qn[h] * egc[h] for h in hs]
    kdb = [(kn[h] * jnp.exp(glc[h] - gcol[h])).astype(BF16) for h in hs]

    if nseq == 1:
        sb = [s_prev[0][h].astype(BF16) for h in hs]
        wq = [jnp.concatenate([sol[h][:, dk:], qg[h]], axis=0).astype(BF16) for h in hs]
        ys = [_dot(wq[h], sb[h]) for h in hs]
        yield
        ub = [(sol[h][:, :dk] - ys[h][:rows_n]).astype(BF16) for h in hs]
        o = [ys[h][rows_n:] + _dot(attn[h], ub[h]) for h in hs]
        yield
        s_new = [[jnp.exp(glc[h][0:1, :]) * s_prev[0][h] + _dot_tn(kdb[h], ub[h]) for h in hs]]
    else:
        js = range(nseq)
        sb = [[s_prev[j][h].astype(BF16) for h in hs] for j in js]
        wq = [[jnp.concatenate([sol[h][j * c:(j + 1) * c, dk:], qg[h][j * c:(j + 1) * c]], axis=0).astype(BF16)
               for h in hs] for j in js]
        ys = [[_dot(wq[j][h], sb[j][h]) for h in hs] for j in js]
        ws = [jnp.concatenate([ys[j][h][:c] for j in js], axis=0) for h in hs]
        qs = [jnp.concatenate([ys[j][h][c:] for j in js], axis=0) for h in hs]
        u = [sol[h][:, :dk] - ws[h] for h in hs]
        o = [qs[h] + _dot(attn[h], u[h].astype(BF16)) for h in hs]
        seq_of_row = lax.broadcasted_iota(jnp.int32, (rows_n, dk), 0) >> (c.bit_length() - 1)
        s_new = [[None] * heads for _ in js]
        for j in range(0, nseq, 2):
            um = [jnp.concatenate([jnp.where(seq_of_row == j, u[h], 0.0),
                                   jnp.where(seq_of_row == j + 1, u[h], 0.0)], axis=1).astype(BF16) for h in hs]
            sn = [_dot_tn(kdb[h], um[h]) for h in hs]
            for h in hs:
                s_new[j][h] = jnp.exp(glc[h][j * c:j * c + 1, :]) * s_prev[j][h] + sn[h][:, :dk]
                s_new[j + 1][h] = (jnp.exp(glc[h][(j + 1) * c:(j + 1) * c + 1, :]) * s_prev[j + 1][h]
                                   + sn[h][:, dk:])
    out = []
    for h in hs:
        on = o[h] * lax.rsqrt(jnp.mean(o[h] * o[h], axis=-1, keepdims=True) + EPS) * hng
        out.append(on * _silu(z_ref[:, h * dk:(h + 1) * dk]))
    return out, s_new


def _lockstep(gens):
    results = [None] * len(gens)
    live = list(range(len(gens)))
    while live:
        for idx in list(live):
            try:
                next(gens[idx])
            except StopIteration as stop:
                results[idx] = stop.value
                live.remove(idx)
    return results


def _delta_prompt_kernel(qkv_ref, z_ref, ba_ref, wconv_ref, alog_ref, dtb_ref, hng_ref,
                         o_ref, s_ref, qbuf_ref, ext_ref, *, c, nb, heads, dk):
    i = pl.program_id(1)

    @pl.when(i == 0)
    def _():
        ext_ref[...] = jnp.zeros(ext_ref.shape, F32)
        s_ref[...] = jnp.zeros(s_ref.shape, F32)

    gens = []
    for g in range(nb):
        xe = jnp.concatenate([ext_ref[g], qkv_ref[g]], axis=0)
        tiles = xe.reshape((QKV_HIST + c) // SUBLANE, SUBLANE, xe.shape[1])
        acc = _short_conv(tiles, lambda a: a[1:], lambda a: a[:-1], wconv_ref).reshape(c, xe.shape[1])
        ext_ref[g] = qkv_ref[g, c - QKV_HIST:c, :]
        s_prev = [[s_ref[g, h] for h in range(heads)]]
        gens.append(_delta_rows(_silu(acc), z_ref.at[g], ba_ref[g], alog_ref[0], dtb_ref[0], hng_ref[0],
                                s_prev, c=c, nseq=1, heads=heads, dk=dk))
    for g, (out, s_new) in enumerate(_lockstep(gens)):
        for h in range(heads):
            o_ref[g, :, h * dk:(h + 1) * dk] = out[h].astype(o_ref.dtype)
            s_ref[g, h] = s_new[0][h]

    @pl.when(i == pl.num_programs(1) - 1)
    def _():
        qbuf_ref[...] = qkv_ref[:, c - (QKV_CONV - 1):c, :]


def _layer_spec(arr, layer):
    zeros = (0,) * (arr.ndim - 1)
    return pl.BlockSpec((1,) + arr.shape[1:], lambda *_: (layer,) + zeros)


def _delta_prompt(proj, ba, wconv, alog, dtb, hng, layer, *, b, t, nb, heads, dk):
    dd = heads * dk
    c = min(CHUNK, t)
    nc = t // c
    proj3 = proj.reshape(b, t, proj.shape[1])
    kern = functools.partial(_delta_prompt_kernel, c=c, nb=nb, heads=heads, dk=dk)
    o, s_new, qbuf = pl.pallas_call(
        kern,
        out_shape=(jax.ShapeDtypeStruct((b, t, dd), BF16),
                   jax.ShapeDtypeStruct((b, heads, dk, dk), F32),
                   jax.ShapeDtypeStruct((b, QKV_CONV - 1, 3 * dd), F32)),
        grid=(b // nb, nc),
        in_specs=[pl.BlockSpec((nb, c, 3 * dd), lambda i, j: (i, j, 0)),
                  pl.BlockSpec((nb, c, dd), lambda i, j: (i, j, 3)),
                  pl.BlockSpec((nb, c, LANE), lambda i, j: (i, j, 0)),
                  _layer_spec(wconv, layer), _layer_spec(alog, layer), _layer_spec(dtb, layer),
                  _layer_spec(hng, layer)],
        out_specs=(pl.BlockSpec((nb, c, dd), lambda i, j: (i, j, 0)),
                   pl.BlockSpec((nb, heads, dk, dk), lambda i, j: (i, 0, 0, 0)),
                   pl.BlockSpec((nb, QKV_CONV - 1, 3 * dd), lambda i, j: (i, 0, 0))),
        scratch_shapes=[pltpu.VMEM((nb, QKV_HIST, 3 * dd), F32)],
        compiler_params=_params(2),
        name="delta_prompt",
    )(proj3, proj3, ba.reshape(b, t, LANE), wconv, alog, dtb, hng)
    return o.reshape(b * t, dd), s_new, qbuf


def _write_slab(ref, slab, value):
    for other in range(ref.shape[0]):
        if other != slab:
            ref[other] = jnp.zeros(ref.shape[1:], ref.dtype)
    ref[slab] = value


def _delta_sample_kernel(qkv_ref, z_ref, ba_ref, hist_ref, s0_ref, wconv_ref, alog_ref, dtb_ref,
                         hng_ref, *rest, t, nseq, heads, dk, slab):
    o_ref, s_ref, qbuf_ref, ext_ref = rest[-4:]
    n = ext_ref.shape[2]
    first = QKV_HIST - (QKV_CONV - 1)
    ext_ref[:, 0:first, :] = jnp.zeros((nseq, first, n), F32)
    for r in range(QKV_CONV - 1):
        ext_ref[:, first + r, :] = hist_ref[0, r]
    ext_ref[:, QKV_HIST:QKV_HIST + t, :] = qkv_ref[...].reshape(nseq, t, n)
    per_seq = (QKV_HIST + t) // SUBLANE
    tiles = ext_ref[...].reshape(nseq * per_seq, SUBLANE, n)

    def pick(a, lo, hi):
        return a.reshape(nseq, per_seq, SUBLANE, n)[:, lo:hi].reshape(nseq * (per_seq - 1), SUBLANE, n)

    acc = _short_conv(tiles, lambda a: pick(a, 1, per_seq), lambda a: pick(a, 0, per_seq - 1),
                      wconv_ref).reshape(nseq * t, n)
    _write_slab(qbuf_ref, slab, jnp.stack(
        [ext_ref[:, QKV_HIST + t - (QKV_CONV - 1) + r, :] for r in range(QKV_CONV - 1)]))

    s_prev = [[s0_ref[0, j, h] for h in range(heads)] for j in range(nseq)]
    (out, s_new), = _lockstep([_delta_rows(_silu(acc), z_ref, ba_ref[...], alog_ref[0], dtb_ref[0],
                                           hng_ref[0], s_prev, c=t, nseq=nseq, heads=heads, dk=dk)])
    for other in range(s_ref.shape[0]):
        if other != slab:
            s_ref[other] = jnp.zeros(s_ref.shape[1:], F32)
    for h in range(heads):
        o_ref[:, h * dk:(h + 1) * dk] = out[h].astype(o_ref.dtype)
        for j in range(nseq):
            s_ref[slab, j, h] = s_new[j][h]


def _layered_out(depth, layer, prev, block_tail, n_in, batch_pos=0):
    def index(first, i):
        tail = [0] * len(block_tail)
        tail[batch_pos] = i
        return (first,) + tuple(tail)

    if prev is None:
        return pl.BlockSpec((depth,) + block_tail, lambda i: index(0, i)), layer, None
    return pl.BlockSpec((1,) + block_tail, lambda i: index(layer, i)), 0, n_in


def _delta_sample(proj, ba, hist, state, prev, layer, wconv, alog, dtb, hng, *, b, t, nseq, heads, dk):
    dd = heads * dk
    depth = state.shape[0]
    rows_n = nseq * t
    in_specs = [pl.BlockSpec((rows_n, 3 * dd), lambda i: (i, 0)),
                pl.BlockSpec((rows_n, dd), lambda i: (i, 3)),
                pl.BlockSpec((rows_n, LANE), lambda i: (i, 0)),
                pl.BlockSpec((1, QKV_CONV - 1, nseq, 3 * dd), lambda i: (layer, 0, i, 0)),
                pl.BlockSpec((1, nseq, heads, dk, dk), lambda i: (layer, i, 0, 0, 0)),
                _layer_spec(wconv, layer), _layer_spec(alog, layer), _layer_spec(dtb, layer),
                _layer_spec(hng, layer)]
    args = [proj, proj, ba, hist, state, wconv, alog, dtb, hng]
    s_spec, slab, alias_at = _layered_out(depth, layer, prev, (nseq, heads, dk, dk), len(args))
    q_spec, _, _ = _layered_out(depth, layer, prev, (QKV_CONV - 1, nseq, 3 * dd), len(args), batch_pos=1)
    aliases = {}
    if prev is not None:
        in_specs += [pl.BlockSpec(memory_space=pl.ANY)] * 2
        args += list(prev)
        aliases = {alias_at: 1, alias_at + 1: 2}
    kern = functools.partial(_delta_sample_kernel, t=t, nseq=nseq, heads=heads, dk=dk, slab=slab)
    return pl.pallas_call(
        kern,
        out_shape=(jax.ShapeDtypeStruct((b * t, dd), BF16),
                   jax.ShapeDtypeStruct((depth, b, heads, dk, dk), F32),
                   jax.ShapeDtypeStruct((depth, QKV_CONV - 1, b, 3 * dd), F32)),
        grid=(b // nseq,),
        in_specs=in_specs,
        out_specs=(pl.BlockSpec((rows_n, dd), lambda i: (i, 0)), s_spec, q_spec),
        scratch_shapes=[pltpu.VMEM((nseq, QKV_HIST + t, 3 * dd), F32)],
        input_output_aliases=aliases,
        compiler_params=_params(1),
        name="delta_sample",
    )(*args)


def _conf_epilogue(conv, zc, bdw, lng, lnb):
    y = conv + bdw
    mu = jnp.mean(y, axis=-1, keepdims=True)
    yc = y - mu
    var = jnp.mean(yc * yc, axis=-1, keepdims=True)
    y = yc * lax.rsqrt(var + EPS) * lng + lnb
    return _silu(y) * _silu(zc)


def _conf_tile(ga_ref, gb_ref, zc_ref, w_ref, bdw_ref, lng_ref, lnb_ref, ext_ref, sh_ref, conv_ref,
               *, tt, rr):
    dc = ext_ref.shape[1]
    first = GLU_HIST - (CONF_KERNEL - 1)
    ext_ref[GLU_HIST:GLU_HIST + tt, :] = _glu(ga_ref[...], gb_ref[...])
    for p in range(SUBLANE):
        for lt in range(dc // LANE):
            sh_ref[p, lt] = ext_ref[p:p + GLU_HIST + tt, lt * LANE:(lt + 1) * LANE]
    for lt in range(dc // LANE):
        lanes = slice(lt * LANE, (lt + 1) * LANE)
        taps = [jnp.broadcast_to(w_ref[0, j:j + 1, lanes], (rr, LANE)) for j in range(CONF_KERNEL)]

        def body(r, carry, lt=lt, lanes=lanes, taps=taps):
            r0 = pl.multiple_of(r * rr, rr)
            parts = [None] * CONF_PARTIALS
            for j in range(CONF_KERNEL):
                off = first + j
                win = sh_ref[off % SUBLANE, lt, pl.ds(r0 + (off // SUBLANE) * SUBLANE, rr), :]
                p = j % CONF_PARTIALS
                parts[p] = taps[j] * win if parts[p] is None else parts[p] + taps[j] * win
            acc = parts[0]
            for p in range(1, CONF_PARTIALS):
                acc = acc + parts[p]
            conv_ref[pl.ds(r0, rr), lanes] = acc
            return carry

        lax.fori_loop(0, tt // rr, body, 0)
    y = _conf_epilogue(conv_ref[...], zc_ref[...], bdw_ref[0], lng_ref[0], lnb_ref[0])
    ext_ref[0:GLU_HIST, :] = ext_ref[tt:tt + GLU_HIST, :]
    return y


def _conf_prompt_kernel(ga_ref, gb_ref, zc_ref, w_ref, bdw_ref, lng_ref, lnb_ref,
                        u_ref, gbuf_ref, ext_ref, sh_ref, conv_ref, *, tt, rr):
    i = pl.program_id(1)
    dc = ext_ref.shape[1]

    @pl.when(i == 0)
    def _():
        ext_ref[0:GLU_HIST, :] = jnp.zeros((GLU_HIST, dc), F32)
        ext_ref[GLU_HIST + tt:GLU_HIST + tt + SUBLANE, :] = jnp.zeros((SUBLANE, dc), F32)

    y = _conf_tile(ga_ref, gb_ref, zc_ref, w_ref, bdw_ref, lng_ref, lnb_ref, ext_ref, sh_ref, conv_ref,
                   tt=tt, rr=rr)
    u_ref[...] = y.astype(u_ref.dtype)

    @pl.when(i == pl.num_programs(1) - 1)
    def _():
        gbuf_ref[0] = ext_ref[GLU_HIST - (CONF_KERNEL - 1):GLU_HIST, :]


def _conf_prompt(proj, w_dw, b_dw, ln_g, ln_b, layer, *, b, t, dd, dc, tt):
    nt = t // tt
    rr = min(32, tt)
    col0 = (4 * dd) // dc
    kern = functools.partial(_conf_prompt_kernel, tt=tt, rr=rr)
    row = lambda i, j: i * nt + j
    return pl.pallas_call(
        kern,
        out_shape=(jax.ShapeDtypeStruct((b * t, dc), BF16),
                   jax.ShapeDtypeStruct((b, CONF_KERNEL - 1, dc), F32)),
        grid=(b, nt),
        in_specs=[pl.BlockSpec((tt, dc), lambda i, j: (row(i, j), col0)),
                  pl.BlockSpec((tt, dc), lambda i, j: (row(i, j), col0 + 1)),
                  pl.BlockSpec((tt, dc), lambda i, j: (row(i, j), col0 + 2)),
                  _layer_spec(w_dw, layer), _layer_spec(b_dw, layer), _layer_spec(ln_g, layer),
                  _layer_spec(ln_b, layer)],
        out_specs=(pl.BlockSpec((tt, dc), lambda i, j: (row(i, j), 0)),
                   pl.BlockSpec((1, CONF_KERNEL - 1, dc), lambda i, j: (i, 0, 0))),
        scratch_shapes=[pltpu.VMEM((GLU_HIST + tt + SUBLANE, dc), F32),
                        pltpu.VMEM((SUBLANE, dc // LANE, GLU_HIST + tt, LANE), F32),
                        pltpu.VMEM((tt, dc), F32)],
        compiler_params=_params(2),
        name="conf_prompt",
    )(proj, proj, proj, w_dw, b_dw, ln_g, ln_b)


def _conf_sample_kernel(ga_ref, gb_ref, zc_ref, hist_ref, w_ref, bdw_ref, lng_ref, lnb_ref,
                        *rest, t, bb, slab):
    u_ref, gbuf_ref, ext_ref, conv_ref, glu_ref = rest[-5:]
    dc = conv_ref.shape[2]
    hist = CONF_KERNEL - 1
    for other in range(gbuf_ref.shape[0]):
        if other != slab:
            gbuf_ref[other] = jnp.zeros(gbuf_ref.shape[1:], F32)
    glu_ref[...] = _glu(ga_ref[...], gb_ref[...])
    for lt in range(dc // LANE):
        lanes = slice(lt * LANE, (lt + 1) * LANE)
        ext_ref[lt, 0:hist] = hist_ref[0, :, :, lanes]
        for step in range(t):
            ext_ref[lt, hist + step] = glu_ref[:, step, lanes]
        acc = None
        for j in range(CONF_KERNEL):
            term = w_ref[0, j:j + 1, lanes] * ext_ref[lt, j:j + t]
            acc = term if acc is None else acc + term
        conv_ref[:, :, lanes] = acc
        gbuf_ref[slab, :, :, lanes] = ext_ref[lt, t:t + hist]
    zc = jnp.stack([zc_ref[:, step, :] for step in range(t)])
    y = _conf_epilogue(conv_ref[...], zc, bdw_ref[0], lng_ref[0], lnb_ref[0])
    for step in range(t):
        u_ref[:, step, :] = y[step]


def _conf_sample(proj, hist, prev, layer, w_dw, b_dw, ln_g, ln_b, *, b, t, dd, dc, bb):
    depth = hist.shape[0]
    col0 = (4 * dd) // dc
    proj = proj.reshape(b, t, proj.shape[1])
    in_specs = [pl.BlockSpec((bb, t, dc), lambda i: (i, 0, col0)),
                pl.BlockSpec((bb, t, dc), lambda i: (i, 0, col0 + 1)),
                pl.BlockSpec((bb, t, dc), lambda i: (i, 0, col0 + 2)),
                pl.BlockSpec((1, CONF_KERNEL - 1, bb, dc), lambda i: (layer, 0, i, 0)),
                _layer_spec(w_dw, layer), _layer_spec(b_dw, layer), _layer_spec(ln_g, layer),
                _layer_spec(ln_b, layer)]
    args = [proj, proj, proj, hist, w_dw, b_dw, ln_g, ln_b]
    g_spec, slab, alias_at = _layered_out(depth, layer, prev, (CONF_KERNEL - 1, bb, dc), len(args),
                                          batch_pos=1)
    aliases = {}
    if prev is not None:
        in_specs.append(pl.BlockSpec(memory_space=pl.ANY))
        args.append(prev)
        aliases = {alias_at: 1}
    kern = functools.partial(_conf_sample_kernel, t=t, bb=bb, slab=slab)
    u, gbuf = pl.pallas_call(
        kern,
        out_shape=(jax.ShapeDtypeStruct((b, t, dc), F32),
                   jax.ShapeDtypeStruct((depth, CONF_KERNEL - 1, b, dc), F32)),
        grid=(b // bb,),
        in_specs=in_specs,
        out_specs=(pl.BlockSpec((bb, t, dc), lambda i: (i, 0, 0)), g_spec),
        scratch_shapes=[pltpu.VMEM((dc // LANE, CONF_KERNEL - 1 + t, bb, LANE), F32),
                        pltpu.VMEM((t, bb, dc), F32),
                        pltpu.VMEM((bb, t, dc), F32)],
        input_output_aliases=aliases,
        compiler_params=_params(1),
        name="conf_sample",
    )(*args)
    return u.reshape(b * t, dc), gbuf


def _outproj_kernel(x_ref, gate_ref, o_ref, u_ref, wo_ref, wu_ref, fg_ref, y_ref, *, final):
    mix = (_dot(o_ref[...], wo_ref[0].astype(BF16))
           + _dot(u_ref[...].astype(BF16), wu_ref[0].astype(BF16)))
    x = x_ref[...]
    y = x + _per_sequence(gate_ref) * mix.reshape(x.shape)
    if final:
        y = y * lax.rsqrt(jnp.mean(y * y, axis=-1, keepdims=True) + EPS) * fg_ref[...]
    y_ref[...] = y


def _outproj(x, mod, o, u, w_out, final_g, layer, *, bb, tt, final):
    b, t, d = x.shape
    dd, dc = o.shape[1], u.shape[1]
    nt = t // tt
    tm = bb * tt
    kern = functools.partial(_outproj_kernel, final=final)
    return pl.pallas_call(
        kern,
        out_shape=jax.ShapeDtypeStruct((b, t, d), F32),
        grid=(b // bb, nt),
        in_specs=[pl.BlockSpec((bb, tt, d), lambda i, j: (i, j, 0)),
                  _mod_spec(mod, bb, d, 2),
                  pl.BlockSpec((tm, dd), lambda i, j: (i * nt + j, 0)),
                  pl.BlockSpec((tm, dc), lambda i, j: (i * nt + j, 0)),
                  pl.BlockSpec((1, dd, d), lambda i, j: (layer, 0, 0),
                               pipeline_mode=pl.Buffered(1)),
                  pl.BlockSpec((1, dc, d), lambda i, j: (layer, dd // dc, 0),
                               pipeline_mode=pl.Buffered(1)),
                  pl.BlockSpec((1, d), lambda i, j: (0, 0))],
        out_specs=pl.BlockSpec((bb, tt, d), lambda i, j: (i, j, 0)),
        compiler_params=_params(2),
        name="out_proj",
    )(x, mod, o, u, w_out, w_out, final_g.reshape(1, d))


def kernel(x_prompt, x_sample, c_prompt, c_sample, state_delta, state_qkv_conv, state_glu_conv,
           norm_g, w_ada, b_ada, w_in, w_qkv_conv, a_log, dt_bias, head_norm_g, w_dw, b_dw,
           ln_g, ln_b, w_out, final_g):
    depth = w_in.shape[0]
    bp, tp, d = x_prompt.shape
    bs, ts, _ = x_sample.shape
    heads = a_log.shape[1]
    dd = w_qkv_conv.shape[2] // 3
    dk = dd // heads
    dc = w_dw.shape[2]
    assert ts % SUBLANE == 0 and ts >= QKV_CONV - 1 and ts <= CONF_KERNEL - 1
    assert 2 * heads <= LANE and dd % dc == 0 and tp % CHUNK == 0
    assert CHUNK % ts == 0 and bs % (CHUNK // ts) == 0 and (CHUNK // ts) % 2 == 0

    mod = _ada(jnp.concatenate([c_prompt, c_sample], axis=0), w_ada, b_ada)

    tn = min(1024, dd, dc)
    tm = 1024
    tt_in = min(tm, tp)
    bb_s = min(bs, tm // ts)
    tt_out = min(512, tp)
    bb_out = min(bs, 512 // ts)
    tt_conf = min(256, tp)
    bb_conf = min(bs, 16)
    nb_delta = 4 if bp % 4 == 0 else 1
    nseq = CHUNK // ts

    w_t = jnp.swapaxes(w_in, 1, 2)
    w_ba = jnp.pad(w_t[:, 4 * dd:4 * dd + 2 * heads], ((0, 0), (0, LANE - 2 * heads), (0, 0)))
    w_out_b = w_out
    qkv_hist = jnp.swapaxes(state_qkv_conv, 1, 2)
    glu_hist = jnp.swapaxes(state_glu_conv, 1, 2)
    norm_g3 = norm_g.reshape(depth, 1, d)
    alog = jnp.pad(a_log, ((0, 0), (heads, LANE - 2 * heads))).reshape(depth, 1, LANE)
    dtb = jnp.pad(dt_bias, ((0, 0), (heads, LANE - 2 * heads))).reshape(depth, 1, LANE)
    hng = head_norm_g.reshape(depth, 1, dk)
    bdw, lng, lnb = b_dw.reshape(depth, 1, dc), ln_g.reshape(depth, 1, dc), ln_b.reshape(depth, 1, dc)

    hp, hs = x_prompt, x_sample
    sp_l, qp_l, gp_l = [], [], []
    s_s = q_s = g_s = None
    for l in range(depth):
        mod_p = mod[l, :bp].reshape(bp, 1, 3 * d)
        mod_s = mod[l, bp:]
        final = l == depth - 1

        proj, ba = _inproj(hp, mod_p, norm_g3, w_t, w_ba, l, bb=1, tt=tt_in, tn=tn,
                           n_a=4 * dd // tn, skip=2 * heads)
        o, s_new, qbuf = _delta_prompt(proj, ba, w_qkv_conv, alog, dtb, hng, l,
                                       b=bp, t=tp, nb=nb_delta, heads=heads, dk=dk)
        u, gbuf = _conf_prompt(proj, w_dw, bdw, lng, lnb, l, b=bp, t=tp, dd=dd, dc=dc, tt=tt_conf)
        hp = _outproj(hp, mod_p, o, u, w_out_b, final_g, l, bb=1, tt=tt_out, final=final)
        sp_l.append(s_new); qp_l.append(qbuf); gp_l.append(gbuf)

        proj, ba = _inproj(hs, mod_s, norm_g3, w_t, w_ba, l, bb=bb_s, tt=ts, tn=tn,
                           n_a=4 * dd // tn, skip=2 * heads)
        o, s_s, q_s = _delta_sample(proj, ba, qkv_hist, state_delta,
                                    None if l == 0 else (s_s, q_s), l, w_qkv_conv, alog, dtb, hng,
                                    b=bs, t=ts, nseq=nseq, heads=heads, dk=dk)
        u, g_s = _conf_sample(proj, glu_hist, g_s, l, w_dw, bdw, lng, lnb,
                              b=bs, t=ts, dd=dd, dc=dc, bb=bb_conf)
        hs = _outproj(hs, mod_s, o, u, w_out_b, final_g, l, bb=bb_out, tt=ts, final=final)

    return (hp, hs, jnp.stack(sp_l), jnp.stack(qp_l), jnp.stack(gp_l),
            s_s, jnp.swapaxes(q_s, 1, 2), jnp.swapaxes(g_s, 1, 2))
```
